```python
import jax, jax.numpy as jnp
from jax import lax
import numpy as np

D_MODEL = 1024
BATCH = 2
SEQ = 16384
DEPTH = 2

N_MIXERS = 2
N_META = 16
POOL_WINDOWS = (2, 4, 8, 16)
POOL_GROUPS = len(POOL_WINDOWS)
POOL_GROUP_DIM = D_MODEL // POOL_GROUPS
MAX_WIN = max(POOL_WINDOWS)
HEAD_DIM = 64
N_HEADS = D_MODEL // HEAD_DIM
ATTN_BLOCK = 128
NEG_INF = -1e30
N_EXPERTS = 32
TOP_K = 4
D_EXPERT = D_MODEL
SWIGLU_LIMIT = 7.0
SWIGLU_ALPHA = 1.702
EXPERT_ROW_BLOCK = 256
DEEPNORM_ALPHA = (2 * DEPTH) ** 0.25
DEEPNORM_BETA = (8 * DEPTH) ** -0.25
LN_EPS = 1e-5
N_POOL_LAYERS = (DEPTH + N_MIXERS - 1) // N_MIXERS
N_FOX_LAYERS = DEPTH // N_MIXERS

kernel_name = 'hybrid_pool_fox_moe_deepnorm'


def layer_norm(x, g, b):
    xf = x.astype(jnp.float32)
    mu = jnp.mean(xf, axis=-1, keepdims=True)
    var = jnp.mean(jnp.square(xf - mu), axis=-1, keepdims=True)
    y = (xf - mu) * lax.rsqrt(var + LN_EPS) * g.astype(jnp.float32) + b.astype(jnp.float32)
    return y.astype(x.dtype)


def pool_mixer(x, pool_w, pool_scale):
    bsz, length, _ = x.shape
    xf = x.astype(jnp.float32)
    cs = jnp.cumsum(jnp.pad(xf, ((0, 0), (MAX_WIN + 1, 0), (0, 0))), axis=1)
    hi = cs[:, MAX_WIN + 1:MAX_WIN + 1 + length]
    pos = jnp.arange(length)
    means = []
    for g, w in enumerate(POOL_WINDOWS):
        sl = slice(g * POOL_GROUP_DIM, (g + 1) * POOL_GROUP_DIM)
        lo = cs[:, MAX_WIN + 1 - w:MAX_WIN + 1 - w + length, sl]
        cnt = jnp.minimum(pos + 1, w).astype(jnp.float32)[None, :, None]
        means.append((hi[..., sl] - lo) / cnt)
    u = jnp.concatenate(means, axis=-1) - xf
    u = u.reshape(bsz, length, POOL_GROUPS, POOL_GROUP_DIM)
    y = jnp.einsum('blgc,gcd->blgd', u, pool_w.astype(jnp.float32))
    y = y.reshape(bsz, length, D_MODEL) * pool_scale.astype(jnp.float32)
    return y.astype(x.dtype)


def fox_mixer(x, w_in, b_f, w_out):
    bsz, length, _ = x.shape
    proj = x @ w_in
    q = proj[..., :D_MODEL]
    k = proj[..., D_MODEL:2 * D_MODEL]
    v = proj[..., 2 * D_MODEL:3 * D_MODEL]
    log_f = jax.nn.log_sigmoid((proj[..., 3 * D_MODEL:] + b_f).astype(jnp.float32))
    pad = ATTN_BLOCK - N_META
    padded_len = pad + length
    n_blocks = padded_len // ATTN_BLOCK
    def to_heads(t):
        t = jnp.pad(t, ((0, 0), (pad, 0), (0, 0)))
        return t.reshape(bsz, padded_len, N_HEADS, HEAD_DIM).transpose(0, 2, 1, 3)
    qh, kh, vh = to_heads(q), to_heads(k), to_heads(v)
    c = jnp.cumsum(jnp.pad(log_f, ((0, 0), (pad, 0), (0, 0))), axis=1).transpose(0, 2, 1)
    key_pos = jnp.arange(padded_len)
    scale = HEAD_DIM ** -0.5

    def attend_block(i):
        start = i * ATTN_BLOCK
        qb = lax.dynamic_slice_in_dim(qh, start, ATTN_BLOCK, axis=2)
        cb = lax.dynamic_slice_in_dim(c, start, ATTN_BLOCK, axis=2)
        q_pos = start + jnp.arange(ATTN_BLOCK)
        logits = jnp.einsum('bhqd,bhkd->bhqk', qb, kh).astype(jnp.float32) * scale
        logits = logits + (cb[..., :, None] - c[..., None, :])
        mask = (key_pos[None, :] <= q_pos[:, None]) & (key_pos[None, :] >= pad)
        logits = jnp.where(mask, logits, NEG_INF)
        p = jax.nn.softmax(logits, axis=-1).astype(vh.dtype)
        return jnp.einsum('bhqk,bhkd->bhqd', p, vh)

    o = lax.map(attend_block, jnp.arange(n_blocks))
    o = o.transpose(1, 0, 3, 2, 4).reshape(bsz, padded_len, D_MODEL)[:, pad:]
    return o @ w_out


def clamped_swiglu(h):
    gate = jnp.minimum(h[..., :D_EXPERT], SWIGLU_LIMIT)
    up = jnp.clip(h[..., D_EXPERT:], -SWIGLU_LIMIT, SWIGLU_LIMIT)
    return gate * jax.nn.sigmoid(SWIGLU_ALPHA * gate) * (up + 1.0)


def moe(x, router_w, router_b, w1, b1, w2, b2):
    bsz, length, d = x.shape
    xt = x.reshape(-1, d)
    n_tok = xt.shape[0]
    logits = (xt @ router_w + router_b).astype(jnp.float32)
    top_v, top_i = lax.top_k(logits, TOP_K)
    gates = jax.nn.softmax(top_v, axis=-1).astype(x.dtype)
    n_copies = n_tok * TOP_K
    flat_e = top_i.reshape(-1)
    flat_tok = jnp.arange(n_copies, dtype=jnp.int32) // TOP_K
    flat_g = gates.reshape(-1)
    order = jnp.argsort(flat_e)
    sorted_e = flat_e[order]
    counts = jnp.bincount(flat_e, length=N_EXPERTS)
    starts = jnp.cumsum(counts) - counts
    padded = (counts + EXPERT_ROW_BLOCK - 1) // EXPERT_ROW_BLOCK * EXPERT_ROW_BLOCK
    pad_ends = jnp.cumsum(padded)
    pad_starts = pad_ends - padded
    dest = pad_starts[sorted_e] + (jnp.arange(n_copies) - starts[sorted_e])
    n_blocks = -(-n_copies // EXPERT_ROW_BLOCK) + N_EXPERTS
    n_rows = n_blocks * EXPERT_ROW_BLOCK
    row_tok = jnp.full((n_rows,), n_tok, jnp.int32).at[dest].set(flat_tok[order])
    row_g = jnp.zeros((n_rows,), x.dtype).at[dest].set(flat_g[order])
    block_e = jnp.minimum(
        jnp.searchsorted(pad_ends, jnp.arange(n_blocks) * EXPERT_ROW_BLOCK, side='right'),
        N_EXPERTS - 1)
    x_ext = jnp.concatenate([xt, jnp.zeros((1, d), xt.dtype)], axis=0)

    def expert_rows(args):
        tok, g, e = args
        h = x_ext[tok] @ w1[e] + b1[e]
        return (clamped_swiglu(h) @ w2[e] + b2[e]) * g[:, None]

    y = lax.map(expert_rows, (row_tok.reshape(n_blocks, EXPERT_ROW_BLOCK),
                              row_g.reshape(n_blocks, EXPERT_ROW_BLOCK), block_e))
    out = jax.ops.segment_sum(y.reshape(n_rows, d), row_tok, num_segments=n_tok + 1)[:n_tok]
    return out.reshape(bsz, length, d)


def setup_inputs(seed: int = 0) -> dict:
    key = jax.random.key(seed)
    ks = jax.random.split(key, 15)
    d, e, f, h = D_MODEL, N_EXPERTS, D_EXPERT, N_HEADS
    nrm = jax.random.normal
    x = nrm(ks[0], (BATCH, SEQ, d), jnp.float32)
    meta_tokens = nrm(ks[1], (N_META, d), jnp.float32)
    pool_w = nrm(ks[2], (N_POOL_LAYERS, POOL_GROUPS, POOL_GROUP_DIM, POOL_GROUP_DIM), jnp.float32) * (POOL_GROUP_DIM ** -0.5 * DEEPNORM_BETA)
    pool_scale = 1.0 + 0.02 * nrm(ks[3], (N_POOL_LAYERS, d), jnp.float32)
    col_scale = jnp.concatenate([jnp.ones((2 * d,), jnp.float32),
                                 jnp.full((d,), DEEPNORM_BETA, jnp.float32),
                                 jnp.ones((h,), jnp.float32)])
    attn_w_in = nrm(ks[4], (N_FOX_LAYERS, d, 3 * d + h), jnp.float32) * (d ** -0.5) * col_scale
    attn_b_f = jax.random.uniform(ks[5], (N_FOX_LAYERS, h), jnp.float32, minval=1.0, maxval=4.0)
    attn_w_out = nrm(ks[6], (N_FOX_LAYERS, d, d), jnp.float32) * (d ** -0.5 * DEEPNORM_BETA)
    ln_g = 1.0 + 0.02 * nrm(ks[7], (DEPTH, 2, d), jnp.float32)
    ln_b = 0.01 * nrm(ks[8], (DEPTH, 2, d), jnp.float32)
    router_w = nrm(ks[9], (DEPTH, d, e), jnp.float32) * (d ** -0.5)
    router_b = 0.01 * nrm(ks[10], (DEPTH, e), jnp.float32)
    w1 = nrm(ks[11], (DEPTH, e, d, 2 * f), jnp.float32) * (d ** -0.5)
    b1 = 0.01 * nrm(ks[12], (DEPTH, e, 2 * f), jnp.float32)
    w2 = nrm(ks[13], (DEPTH, e, f, d), jnp.float32) * (f ** -0.5 * DEEPNORM_BETA)
    b2 = 0.01 * nrm(ks[14], (DEPTH, e, d), jnp.float32)
    return {'x': x, 'meta_tokens': meta_tokens, 'pool_w': pool_w, 'pool_scale': pool_scale,
            'attn_w_in': attn_w_in, 'attn_b_f': attn_b_f, 'attn_w_out': attn_w_out,
            'ln_g': ln_g, 'ln_b': ln_b, 'router_w': router_w, 'router_b': router_b,
            'w1': w1, 'b1': b1, 'w2': w2, 'b2': b2}


def reference(x, meta_tokens, pool_w, pool_scale, attn_w_in, attn_b_f, attn_w_out,
              ln_g, ln_b, router_w, router_b, w1, b1, w2, b2):
    bsz = x.shape[0]
    meta = jnp.broadcast_to(meta_tokens.astype(x.dtype)[None], (bsz, N_META, D_MODEL))
    h = jnp.concatenate([meta, x], axis=1)
    for i in range(DEPTH):
        j = i // N_MIXERS
        if i % N_MIXERS == 0:
            mix = pool_mixer(h, pool_w[j], pool_scale[j])
        else:
            mix = fox_mixer(h, attn_w_in[j], attn_b_f[j], attn_w_out[j])
        h = layer_norm(DEEPNORM_ALPHA * h + mix, ln_g[i, 0], ln_b[i, 0])
        ffn = moe(h, router_w[i], router_b[i], w1[i], b1[i], w2[i], b2[i])
        h = layer_norm(DEEPNORM_ALPHA * h + ffn, ln_g[i, 1], ln_b[i, 1])
    return h[:, N_META:]
```

```python
import functools

import numpy as np
import jax
import jax.numpy as jnp
from jax import lax
from jax.experimental import pallas as pl
from jax.experimental.pallas import tpu as pltpu

POOL_WINDOWS = (2, 4, 8, 16)
MAX_WIN = max(POOL_WINDOWS)
TOP_K = 4
SWIGLU_LIMIT = 7.0
SWIGLU_ALPHA = 1.702
LN_EPS = 1e-5
MASK_VALUE = -1e30

LANES = 128
SEQ_TILE = 512
ROUTE_TILE = 256
EXPERT_ROWS = 256
VMEM_LIMIT = 56 * 1024 * 1024

F32 = jnp.float32
BF16 = jnp.bfloat16
HIGHEST = lax.Precision.HIGHEST


def _params(sem, vmem=VMEM_LIMIT):
    return pltpu.CompilerParams(dimension_semantics=sem, vmem_limit_bytes=vmem)


def _layer_norm(z, g, b):
    mu = jnp.mean(z, axis=-1, keepdims=True)
    d = z - mu
    var = jnp.mean(d * d, axis=-1, keepdims=True)
    return d * lax.rsqrt(var + LN_EPS) * g + b


def _route(hn, rw_ref, rb_ref, run_ref, ti_ref, gate_ref, rank_ref, cnt_ref):
    rows = hn.shape[0]
    logits = jnp.dot(hn, rw_ref[...], precision=HIGHEST, preferred_element_type=F32) + rb_ref[...]
    n_exp = logits.shape[1]
    lane = lax.broadcasted_iota(jnp.int32, (rows, n_exp), 1)
    cur = logits
    vals, idxs, hots = [], [], []
    for _ in range(TOP_K):
        m = jnp.max(cur, axis=-1, keepdims=True)
        idx = jnp.min(jnp.where(cur == m, lane, n_exp), axis=-1, keepdims=True)
        hot = lane == idx
        vals.append(m)
        idxs.append(idx)
        hots.append(hot)
        cur = jnp.where(hot, -jnp.inf, cur)
    exps = [jnp.exp(v - vals[0]) for v in vals]
    denom = exps[0]
    for e in exps[1:]:
        denom = denom + e
    gates = [e / denom for e in exps]
    sel = hots[0].astype(F32)
    for hot in hots[1:]:
        sel = sel + hot.astype(F32)
    r = lax.broadcasted_iota(jnp.int32, (rows, rows), 0)
    c = lax.broadcasted_iota(jnp.int32, (rows, rows), 1)
    tri = (c < r).astype(BF16)
    base = run_ref[...] + jnp.dot(tri, sel.astype(BF16), preferred_element_type=F32)
    ranks = [jnp.sum(jnp.where(hot, base, 0.0), axis=-1, keepdims=True) for hot in hots]
    run_ref[...] = run_ref[...] + jnp.sum(sel, axis=0, keepdims=True)
    cnt_ref[...] = run_ref[...]
    lane_k = lax.broadcasted_iota(jnp.int32, (rows, TOP_K), 1)

    def pack(cols):
        out = jnp.broadcast_to(cols[TOP_K - 1], (rows, TOP_K))
        for k in range(TOP_K - 2, -1, -1):
            out = jnp.where(lane_k == k, cols[k], out)
        return out

    ti_ref[...] = pack(idxs)
    gate_ref[...] = pack(gates)
    rank_ref[...] = pack(ranks).astype(jnp.int32)


def _pool_kernel(h_ref, pw_ref, ps_ref, g_ref, b_ref, rw_ref, rb_ref,
                 h1_ref, ti_ref, gate_ref, rank_ref, cnt_ref,
                 ext_ref, run_ref, *, tile, alpha):
    bi = pl.program_id(0)
    i = pl.program_id(1)
    d_model = h_ref.shape[1]
    gdim = d_model // len(POOL_WINDOWS)

    @pl.when(i == 0)
    def _():
        ext_ref[0:MAX_WIN, :] = jnp.zeros((MAX_WIN, d_model), F32)

    @pl.when((bi == 0) & (i == 0))
    def _():
        run_ref[...] = jnp.zeros_like(run_ref)

    x = h_ref[...]
    ext_ref[MAX_WIN:MAX_WIN + tile, :] = x
    pos = i * tile + lax.broadcasted_iota(jnp.int32, (tile, 1), 0)
    ys = []
    for g, w in enumerate(POOL_WINDOWS):
        lo, hi = g * gdim, (g + 1) * gdim
        xg = x[:, lo:hi]
        s = xg
        for j in range(1, w):
            s = s + ext_ref[MAX_WIN - j:MAX_WIN - j + tile, lo:hi]
        cnt = jnp.minimum(pos + 1, w).astype(F32)
        u = s / cnt - xg
        ys.append(jnp.dot(u.astype(BF16), pw_ref[g].astype(BF16), preferred_element_type=F32))
    y = jnp.concatenate(ys, axis=-1) * ps_ref[...]
    hn = _layer_norm(alpha * x + y, g_ref[...], b_ref[...])
    h1_ref[...] = hn
    ext_ref[0:MAX_WIN, :] = x[tile - MAX_WIN:tile, :]
    _route(hn, rw_ref, rb_ref, run_ref, ti_ref, gate_ref, rank_ref, cnt_ref)


def _route_out_shapes(tp, n_exp):
    return [jax.ShapeDtypeStruct((tp, TOP_K), jnp.int32),
            jax.ShapeDtypeStruct((tp, TOP_K), F32),
            jax.ShapeDtypeStruct((tp, TOP_K), jnp.int32),
            jax.ShapeDtypeStruct((1, n_exp), F32)]


def _route_out_specs(nt, tile, n_exp):
    row = lambda b, i: (b * nt + i, 0)
    return [pl.BlockSpec((tile, TOP_K), row),
            pl.BlockSpec((tile, TOP_K), row),
            pl.BlockSpec((tile, TOP_K), row),
            pl.BlockSpec((1, n_exp), lambda b, i: (0, 0))]


def _const_spec(shape):
    return pl.BlockSpec(shape, lambda b, i: (0,) * len(shape))


def _pool_call(h, pool_w, pool_scale, ln_g, ln_b, router_w, router_b, *, bsz, lp, alpha):
    tp, d = h.shape
    tile = SEQ_TILE
    nt = lp // tile
    n_exp = router_w.shape[1]
    groups, gdim, _ = pool_w.shape
    row = lambda b, i: (b * nt + i, 0)
    return pl.pallas_call(
        functools.partial(_pool_kernel, tile=tile, alpha=alpha),
        grid=(bsz, nt),
        in_specs=[pl.BlockSpec((tile, d), row),
                  _const_spec((groups, gdim, gdim)),
                  _const_spec((1, d)), _const_spec((1, d)), _const_spec((1, d)),
                  _const_spec((d, n_exp)), _const_spec((1, n_exp))],
        out_specs=[pl.BlockSpec((tile, d), row)] + _route_out_specs(nt, tile, n_exp),
        out_shape=[jax.ShapeDtypeStruct((tp, d), F32)] + _route_out_shapes(tp, n_exp),
        scratch_shapes=[pltpu.VMEM((MAX_WIN + tile, d), F32), pltpu.VMEM((1, n_exp), F32)],
        compiler_params=_params(("arbitrary", "arbitrary")),
    )(h, pool_w, pool_scale.reshape(1, d), ln_g.reshape(1, d), ln_b.reshape(1, d),
      router_w, router_b.reshape(1, n_exp))


def _plan(counts, top_i, rank, n_blocks):
    n_exp = counts.shape[1]
    cnt = counts.reshape(n_exp).astype(jnp.int32)
    padded = (cnt + EXPERT_ROWS - 1) // EXPERT_ROWS * EXPERT_ROWS
    pad_ends = jnp.cumsum(padded)
    pad_starts = pad_ends - padded
    hot = top_i[..., None] == jnp.arange(n_exp, dtype=jnp.int32)
    dest = jnp.sum(jnp.where(hot, pad_starts, 0), axis=-1) + rank
    starts = jnp.arange(n_blocks, dtype=jnp.int32) * EXPERT_ROWS
    block_e = jnp.sum((starts[:, None] >= pad_ends[None, :]).astype(jnp.int32), axis=1)
    block_e = jnp.minimum(block_e, n_exp - 1)
    n_used = (pad_ends[-1] // EXPERT_ROWS).reshape(1)
    return dest.reshape(-1).astype(jnp.int32), block_e, n_used


def _dispatch_kernel(dest_ref, h_ref, xs_in_ref, xs_ref, sem, *, tile):
    del xs_in_ref

    def row_copy(r, k, d):
        return pltpu.make_async_copy(h_ref.at[pl.ds(r, 1), :], xs_ref.at[pl.ds(d, 1), :], sem)

    def issue(r, carry):
        for k in range(TOP_K):
            row_copy(r, k, dest_ref[r * TOP_K + k]).start()
        return carry

    lax.fori_loop(0, tile, issue, 0)

    def drain(r, carry):
        for k in range(TOP_K):
            row_copy(0, k, 0).wait()
        return carry

    lax.fori_loop(0, tile, drain, 0)


def _dispatch_call(dest, h, n_rows):
    tp, d = h.shape
    tile = ROUTE_TILE
    xs0 = jnp.zeros((n_rows, d), F32)
    return pl.pallas_call(
        functools.partial(_dispatch_kernel, tile=tile),
        grid=(tp // tile,),
        in_specs=[pl.BlockSpec((tile * TOP_K,), lambda i: (i,), memory_space=pltpu.SMEM),
                  pl.BlockSpec((tile, d), lambda i: (i, 0)),
                  pl.BlockSpec(memory_space=pl.ANY)],
        out_specs=pl.BlockSpec(memory_space=pl.ANY),
        out_shape=jax.ShapeDtypeStruct((n_rows, d), F32),
        scratch_shapes=[pltpu.SemaphoreType.DMA(())],
        input_output_aliases={2: 0},
        compiler_params=_params(("arbitrary",)),
    )(dest, h, xs0)


def _ffn_kernel(be_ref, nu_ref, x_ref, w1_ref, b1_ref, w2_ref, b2_ref, y_ref, w1b_ref, w2b_ref):
    i = pl.program_id(0)
    f = w2_ref.shape[1]
    active = i < nu_ref[0]
    changed = (i == 0) | (be_ref[i] != be_ref[jnp.maximum(i - 1, 0)])

    @pl.when(active & changed)
    def _():
        w1b_ref[...] = w1_ref[0].astype(BF16)
        w2b_ref[...] = w2_ref[0].astype(BF16)

    @pl.when(active)
    def _():
        x = x_ref[...].astype(BF16)
        hid = jnp.dot(x, w1b_ref[...], preferred_element_type=F32) + b1_ref[0]
        gate = jnp.minimum(hid[:, :f], SWIGLU_LIMIT)
        up = jnp.clip(hid[:, f:], -SWIGLU_LIMIT, SWIGLU_LIMIT)
        act = gate * jax.nn.sigmoid(SWIGLU_ALPHA * gate) * (up + 1.0)
        y_ref[...] = jnp.dot(act.astype(BF16), w2b_ref[...], preferred_element_type=F32) + b2_ref[0]

    @pl.when(jnp.logical_not(active))
    def _():
        y_ref[...] = jnp.zeros_like(y_ref)


def _ffn_call(block_e, n_used, xs, w1, b1, w2, b2):
    n_rows, d = xs.shape
    n_exp, _, f2 = w1.shape
    f = w2.shape[1]
    nb = n_rows // EXPERT_ROWS

    def blk(i, be, nu):
        return jnp.minimum(i, nu[0] - 1)

    grid_spec = pltpu.PrefetchScalarGridSpec(
        num_scalar_prefetch=2,
        grid=(nb,),
        in_specs=[pl.BlockSpec((EXPERT_ROWS, d), lambda i, be, nu: (blk(i, be, nu), 0)),
                  pl.BlockSpec((1, d, f2), lambda i, be, nu: (be[blk(i, be, nu)], 0, 0)),
                  pl.BlockSpec((1, 1, f2), lambda i, be, nu: (be[blk(i, be, nu)], 0, 0)),
                  pl.BlockSpec((1, f, d), lambda i, be, nu: (be[blk(i, be, nu)], 0, 0)),
                  pl.BlockSpec((1, 1, d), lambda i, be, nu: (be[blk(i, be, nu)], 0, 0))],
        out_specs=pl.BlockSpec((EXPERT_ROWS, d), lambda i, be, nu: (i, 0)),
        scratch_shapes=[pltpu.VMEM((d, f2), BF16), pltpu.VMEM((f, d), BF16)],
    )
    return pl.pallas_call(
        _ffn_kernel,
        grid_spec=grid_spec,
        out_shape=jax.ShapeDtypeStruct((n_rows, d), F32),
        compiler_params=_params(("arbitrary",)),
    )(block_e, n_used, xs, w1, b1.reshape(n_exp, 1, f2), w2, b2.reshape(n_exp, 1, d))


def _combine_kernel(dest_ref, gate_ref, h_ref, g_ref, b_ref, y_ref, out_ref, buf_ref, sem, *, tile, alpha):
    def row_copy(r, k, d):
        return pltpu.make_async_copy(y_ref.at[pl.ds(d, 1), :], buf_ref.at[k, pl.ds(r, 1), :], sem)

    def issue(r, carry):
        for k in range(TOP_K):
            row_copy(r, k, dest_ref[r * TOP_K + k]).start()
        return carry

    lax.fori_loop(0, tile, issue, 0)

    def drain(r, carry):
        for k in range(TOP_K):
            row_copy(0, k, 0).wait()
        return carry

    lax.fori_loop(0, tile, drain, 0)

    gates = gate_ref[...]
    ffn = gates[:, 0:1] * buf_ref[0]
    for k in range(1, TOP_K):
        ffn = ffn + gates[:, k:k + 1] * buf_ref[k]
    out_ref[...] = _layer_norm(alpha * h_ref[...] + ffn, g_ref[...], b_ref[...])


def _combine_call(dest, gates, h, y, ln_g, ln_b, *, alpha):
    tp, d = h.shape
    tile = ROUTE_TILE
    return pl.pallas_call(
        functools.partial(_combine_kernel, tile=tile, alpha=alpha),
        grid=(tp // tile,),
        in_specs=[pl.BlockSpec((tile * TOP_K,), lambda i: (i,), memory_space=pltpu.SMEM),
                  pl.BlockSpec((tile, TOP_K), lambda i: (i, 0)),
                  pl.BlockSpec((tile, d), lambda i: (i, 0)),
                  pl.BlockSpec((1, d), lambda i: (0, 0)),
                  pl.BlockSpec((1, d), lambda i: (0, 0)),
                  pl.BlockSpec(memory_space=pl.ANY)],
        out_specs=pl.BlockSpec((tile, d), lambda i: (i, 0)),
        out_shape=jax.ShapeDtypeStruct((tp, d), F32),
        scratch_shapes=[pltpu.VMEM((TOP_K, tile, d), F32), pltpu.SemaphoreType.DMA(())],
        compiler_params=_params(("arbitrary",)),
    )(dest, gates, h, ln_g.reshape(1, d), ln_b.reshape(1, d), y)


def _moe(h, top_i, gates, rank, counts, w1, b1, w2, b2, ln_g, ln_b, *, alpha):
    tp = h.shape[0]
    n_exp = w1.shape[0]
    n_blocks = tp * TOP_K // EXPERT_ROWS + n_exp
    dest, block_e, n_used = _plan(counts, top_i, rank, n_blocks)
    xs = _dispatch_call(dest, h, n_blocks * EXPERT_ROWS)
    y = _ffn_call(block_e, n_used, xs, w1, b1, w2, b2)
    return _combine_call(dest, gates, h, y, ln_g, ln_b, alpha=alpha)


def _log_sigmoid(x):
    return jnp.minimum(x, 0.0) - jnp.log(1.0 + jnp.exp(-jnp.abs(x)))


def _split3(c):
    hi = c.astype(BF16)
    r1 = c - hi.astype(F32)
    mid = r1.astype(BF16)
    lo = (r1 - mid.astype(F32)).astype(BF16)
    return hi, mid, lo


def _proj_kernel(h_ref, wqt_ref, wk_ref, wvt_ref, wf_ref, wft_ref, bf_ref, bft_ref,
                 selq_ref, selk_ref, oneq_ref, onek_ref, onev_ref,
                 qt_ref, kx_ref, vt_ref, carry_ref, carryt_ref, *, tile, n_heads):
    i = pl.program_id(1)

    @pl.when(i == 0)
    def _():
        carry_ref[...] = jnp.zeros_like(carry_ref)
        carryt_ref[...] = jnp.zeros_like(carryt_ref)

    x = h_ref[...]
    xb = x.astype(BF16)
    nt_dims = (((1,), (1,)), ((), ()))
    lf = _log_sigmoid(jnp.dot(x, wf_ref[...], precision=HIGHEST, preferred_element_type=F32) + bf_ref[...])
    lft = _log_sigmoid(lax.dot_general(wft_ref[...], x, nt_dims, precision=HIGHEST,
                                       preferred_element_type=F32) + bft_ref[...])
    r = lax.broadcasted_iota(jnp.int32, (tile, tile), 0)
    c = lax.broadcasted_iota(jnp.int32, (tile, tile), 1)
    cs = jnp.dot((c <= r).astype(F32), lf, precision=HIGHEST, preferred_element_type=F32) + carry_ref[...]
    cst = jnp.dot(lft, (r <= c).astype(F32), precision=HIGHEST, preferred_element_type=F32) + carryt_ref[...]
    carry_ref[...] = cs[tile - 1:tile, :]
    carryt_ref[...] = cst[:, tile - 1:tile]

    qt = lax.dot_general(wqt_ref[...], xb, nt_dims, preferred_element_type=F32) + oneq_ref[...]
    for p, part in enumerate(_split3(cst)):
        qt = qt + jnp.dot(selq_ref[p], part, preferred_element_type=F32)
    kx = jnp.dot(xb, wk_ref[...], preferred_element_type=F32) + onek_ref[...]
    for p, part in enumerate(_split3(-cs)):
        kx = kx + jnp.dot(part, selk_ref[p], preferred_element_type=F32)
    vt = lax.dot_general(wvt_ref[...], xb, nt_dims, preferred_element_type=F32) + onev_ref[...]
    for hd in range(n_heads):
        qt_ref[0, hd, 0] = qt[hd * LANES:(hd + 1) * LANES, :].astype(BF16)
        kx_ref[0, hd, 0] = kx[:, hd * LANES:(hd + 1) * LANES].astype(BF16)
        vt_ref[0, hd, 0] = vt[hd * LANES:(hd + 1) * LANES, :].astype(BF16)


def _attn_weights(w_in, b_f, n_heads, head_dim):
    d = w_in.shape[0]
    scale = head_dim ** -0.5
    hw = n_heads * LANES

    def pad_heads(w):
        w = w.reshape(d, n_heads, head_dim)
        return jnp.pad(w, ((0, 0), (0, 0), (0, LANES - head_dim))).reshape(d, hw)

    wqt = pad_heads(w_in[:, :d] * scale).T.astype(BF16)
    wk = pad_heads(w_in[:, d:2 * d]).astype(BF16)
    wvt = pad_heads(w_in[:, 2 * d:3 * d]).T.astype(BF16)
    wf = w_in[:, 3 * d:]
    selq = np.zeros((3, hw, n_heads), np.float32)
    selk = np.zeros((3, n_heads, hw), np.float32)
    oneq = np.zeros((hw, 1), np.float32)
    onek = np.zeros((1, hw), np.float32)
    onev = np.zeros((hw, 1), np.float32)
    for h in range(n_heads):
        base = h * LANES + head_dim
        for p in range(3):
            selq[p, base + p, h] = 1.0
            selk[p, h, base + 3 + p] = 1.0
            oneq[base + 3 + p, 0] = 1.0
            onek[0, base + p] = 1.0
        onev[base, 0] = 1.0
    return (wqt, wk, wvt, wf, wf.T, b_f.reshape(1, n_heads), b_f.reshape(n_heads, 1),
            jnp.asarray(selq, BF16), jnp.asarray(selk, BF16),
            jnp.asarray(oneq), jnp.asarray(onek), jnp.asarray(onev))


def _proj_call(h, weights, *, bsz, lp, n_heads):
    tp, d = h.shape
    tile = SEQ_TILE
    nt = lp // tile
    hw = n_heads * LANES
    row = lambda b, i: (b * nt + i, 0)
    in_specs = [pl.BlockSpec((tile, d), row)] + [_const_spec(w.shape) for w in weights]
    t_spec = pl.BlockSpec((1, n_heads, 1, LANES, tile), lambda b, i: (b, 0, i, 0, 0))
    k_spec = pl.BlockSpec((1, n_heads, 1, tile, LANES), lambda b, i: (b, 0, i, 0, 0))
    return pl.pallas_call(
        functools.partial(_proj_kernel, tile=tile, n_heads=n_heads),
        grid=(bsz, nt),
        in_specs=in_specs,
        out_specs=[t_spec, k_spec, t_spec],
        out_shape=[jax.ShapeDtypeStruct((bsz, n_heads, nt, LANES, tile), BF16),
                   jax.ShapeDtypeStruct((bsz, n_heads, nt, tile, LANES), BF16),
                   jax.ShapeDtypeStruct((bsz, n_heads, nt, LANES, tile), BF16)],
        scratch_shapes=[pltpu.VMEM((1, n_heads), F32), pltpu.VMEM((n_heads, 1), F32)],
        compiler_params=_params(("arbitrary", "arbitrary")),
    )(h, *weights)


def _attn_kernel(qt_ref, kx_ref, vt_ref, o_ref, m_ref, acc_ref, *, tile, head_dim):
    qi = pl.program_id(2)
    qt = qt_ref[0, 0, 0]
    m_ref[...] = jnp.full_like(m_ref, MASK_VALUE)
    acc_ref[...] = jnp.zeros_like(acc_ref)

    def step(kj, masked):
        s = jnp.dot(kx_ref[0, 0, kj], qt, preferred_element_type=F32)
        if masked:
            key = lax.broadcasted_iota(jnp.int32, (tile, tile), 0)
            qry = lax.broadcasted_iota(jnp.int32, (tile, tile), 1)
            s = jnp.where(key <= qry, s, MASK_VALUE)
        m_old = m_ref[...]
        m_new = jnp.maximum(m_old, jnp.max(s, axis=0, keepdims=True))
        p = jnp.exp(s - m_new).astype(BF16)
        acc_ref[...] = jnp.exp(m_old - m_new) * acc_ref[...] + jnp.dot(
            vt_ref[0, 0, kj], p, preferred_element_type=F32)
        m_ref[...] = m_new

    def body(kj, carry):
        step(kj, False)
        return carry

    lax.fori_loop(0, qi, body, 0)
    step(qi, True)
    acc = acc_ref[...]
    o_ref[0] = (acc[:head_dim, :] / acc[head_dim:head_dim + 1, :]).astype(BF16)


def _attn_call(qt, kx, vt, *, head_dim):
    bsz, n_heads, nt, _, tile = qt.shape
    d = n_heads * head_dim
    return pl.pallas_call(
        functools.partial(_attn_kernel, tile=tile, head_dim=head_dim),
        grid=(bsz, n_heads, nt),
        in_specs=[pl.BlockSpec((1, 1, 1, LANES, tile), lambda b, h, i: (b, h, i, 0, 0)),
                  pl.BlockSpec((1, 1, nt, tile, LANES), lambda b, h, i: (b, h, 0, 0, 0)),
                  pl.BlockSpec((1, 1, nt, LANES, tile), lambda b, h, i: (b, h, 0, 0, 0))],
        out_specs=pl.BlockSpec((1, head_dim, tile), lambda b, h, i: (b, h, i)),
        out_shape=jax.ShapeDtypeStruct((bsz, d, nt * tile), BF16),
        scratch_shapes=[pltpu.VMEM((1, tile), F32), pltpu.VMEM((LANES, tile), F32)],
        compiler_params=_params(("arbitrary", "arbitrary", "arbitrary")),
    )(qt, kx, vt)


def _oproj_kernel(o_ref, wo_ref, h_ref, g_ref, b_ref, rw_ref, rb_ref,
                  hn_ref, ti_ref, gate_ref, rank_ref, cnt_ref, run_ref, *, alpha):
    bi = pl.program_id(0)
    i = pl.program_id(1)

    @pl.when((bi == 0) & (i == 0))
    def _():
        run_ref[...] = jnp.zeros_like(run_ref)

    att = lax.dot_general(o_ref[0], wo_ref[...], (((0,), (0,)), ((), ())), preferred_element_type=F32)
    hn = _layer_norm(alpha * h_ref[...] + att, g_ref[...], b_ref[...])
    hn_ref[...] = hn
    _route(hn, rw_ref, rb_ref, run_ref, ti_ref, gate_ref, rank_ref, cnt_ref)


def _oproj_call(o, w_out, h, ln_g, ln_b, router_w, router_b, *, bsz, lp, alpha):
    tp, d = h.shape
    tile = SEQ_TILE
    nt = lp // tile
    n_exp = router_w.shape[1]
    row = lambda b, i: (b * nt + i, 0)
    return pl.pallas_call(
        functools.partial(_oproj_kernel, alpha=alpha),
        grid=(bsz, nt),
        in_specs=[pl.BlockSpec((1, d, tile), lambda b, i: (b, 0, i)),
                  _const_spec((d, d)),
                  pl.BlockSpec((tile, d), row),
                  _const_spec((1, d)), _const_spec((1, d)),
                  _const_spec((d, n_exp)), _const_spec((1, n_exp))],
        out_specs=[pl.BlockSpec((tile, d), row)] + _route_out_specs(nt, tile, n_exp),
        out_shape=[jax.ShapeDtypeStruct((tp, d), F32)] + _route_out_shapes(tp, n_exp),
        scratch_shapes=[pltpu.VMEM((1, n_exp), F32)],
        compiler_params=_params(("arbitrary", "arbitrary")),
    )(o, w_out.astype(BF16), h, ln_g.reshape(1, d), ln_b.reshape(1, d),
      router_w, router_b.reshape(1, n_exp))


def kernel(x, meta_tokens, pool_w, pool_scale, attn_w_in, attn_b_f, attn_w_out,
           ln_g, ln_b, router_w, router_b, w1, b1, w2, b2):
    bsz, seq, d = x.shape
    n_meta = meta_tokens.shape[0]
    depth = ln_g.shape[0]
    n_heads = attn_b_f.shape[-1]
    head_dim = d // n_heads
    alpha = float((2 * depth) ** 0.25)
    length = n_meta + seq
    lp = -(-length // SEQ_TILE) * SEQ_TILE
    assert d % (len(POOL_WINDOWS) * LANES) == 0 and head_dim + 6 <= LANES and depth == 2

    meta = jnp.broadcast_to(meta_tokens[None], (bsz, n_meta, d))
    h = jnp.concatenate([meta, x, jnp.zeros((bsz, lp - length, d), x.dtype)], axis=1)
    h = h.reshape(bsz * lp, d)

    h, top_i, gates, rank, counts = _pool_call(
        h, pool_w[0], pool_scale[0], ln_g[0, 0], ln_b[0, 0], router_w[0], router_b[0],
        bsz=bsz, lp=lp, alpha=alpha)
    h = _moe(h, top_i, gates, rank, counts, w1[0], b1[0], w2[0], b2[0], ln_g[0, 1], ln_b[0, 1], alpha=alpha)

    weights = _attn_weights(attn_w_in[0], attn_b_f[0], n_heads, head_dim)
    qt, kx, vt = _proj_call(h, weights, bsz=bsz, lp=lp, n_heads=n_heads)
    o = _attn_call(qt, kx, vt, head_dim=head_dim)
    h, top_i, gates, rank, counts = _oproj_call(
        o, attn_w_out[0], h, ln_g[1, 0], ln_b[1, 0], router_w[1], router_b[1],
        bsz=bsz, lp=lp, alpha=alpha)
    h = _moe(h, top_i, gates, rank, counts, w1[1], b1[1], w2[1], b2[1], ln_g[1, 1], ln_b[1, 1], alpha=alpha)

    return h.reshape(bsz, lp, d)[:, n_meta:length]
```

```python
import functools

import numpy as np
import jax
import jax.numpy as jnp
from jax import lax
from jax.experimental import pallas as pl
from jax.experimental.pallas import tpu as pltpu

POOL_WINDOWS = (2, 4, 8, 16)
MAX_WIN = max(POOL_WINDOWS)
TOP_K = 4
SWIGLU_LIMIT = 7.0
SWIGLU_ALPHA = 1.702
LN_EPS = 1e-5
MASK_VALUE = -1e30

LANES = 128
SEQ_TILE = 512
ROUTE_TILE = 256
EXPERT_ROWS = 256
VMEM_LIMIT = 56 * 1024 * 1024
LOG2E = 1.4426950408889634
ATTN_DEPTH = 3

F32 = jnp.float32
BF16 = jnp.bfloat16
HIGHEST = lax.Precision.HIGHEST


def _params(sem, vmem=VMEM_LIMIT):
    return pltpu.CompilerParams(dimension_semantics=sem, vmem_limit_bytes=vmem)


def _layer_norm(z, g, b):
    mu = jnp.mean(z, axis=-1, keepdims=True)
    d = z - mu
    var = jnp.mean(d * d, axis=-1, keepdims=True)
    return d * lax.rsqrt(var + LN_EPS) * g + b


def _route(hn, rw_ref, rb_ref, run_ref, ti_ref, gate_ref, rank_ref, cnt_ref):
    rows = hn.shape[0]
    logits = jnp.dot(hn, rw_ref[...], precision=HIGHEST, preferred_element_type=F32) + rb_ref[...]
    n_exp = logits.shape[1]
    lane = lax.broadcasted_iota(jnp.int32, (rows, n_exp), 1)
    cur = logits
    vals, idxs, hots = [], [], []
    for _ in range(TOP_K):
        m = jnp.max(cur, axis=-1, keepdims=True)
        idx = jnp.min(jnp.where(cur == m, lane, n_exp), axis=-1, keepdims=True)
        hot = lane == idx
        vals.append(m)
        idxs.append(idx)
        hots.append(hot)
        cur = jnp.where(hot, -jnp.inf, cur)
    exps = [jnp.exp(v - vals[0]) for v in vals]
    denom = exps[0]
    for e in exps[1:]:
        denom = denom + e
    gates = [e / denom for e in exps]
    sel = hots[0].astype(F32)
    for hot in hots[1:]:
        sel = sel + hot.astype(F32)
    r = lax.broadcasted_iota(jnp.int32, (rows, rows), 0)
    c = lax.broadcasted_iota(jnp.int32, (rows, rows), 1)
    tri = (c < r).astype(BF16)
    base = run_ref[...] + jnp.dot(tri, sel.astype(BF16), preferred_element_type=F32)
    ranks = [jnp.sum(jnp.where(hot, base, 0.0), axis=-1, keepdims=True) for hot in hots]
    run_ref[...] = run_ref[...] + jnp.sum(sel, axis=0, keepdims=True)
    cnt_ref[...] = run_ref[...]
    lane_k = lax.broadcasted_iota(jnp.int32, (rows, TOP_K), 1)

    def pack(cols):
        out = jnp.broadcast_to(cols[TOP_K - 1], (rows, TOP_K))
        for k in range(TOP_K - 2, -1, -1):
            out = jnp.where(lane_k == k, cols[k], out)
        return out

    ti_ref[...] = pack(idxs)
    gate_ref[...] = pack(gates)
    rank_ref[...] = pack(ranks).astype(jnp.int32)


def _pool_kernel(h_ref, pw_ref, ps_ref, g_ref, b_ref, rw_ref, rb_ref,
                 h1_ref, ti_ref, gate_ref, rank_ref, cnt_ref,
                 ext_ref, run_ref, *, tile, alpha):
    bi = pl.program_id(0)
    i = pl.program_id(1)
    d_model = h_ref.shape[1]
    gdim = d_model // len(POOL_WINDOWS)

    @pl.when(i == 0)
    def _():
        ext_ref[0:MAX_WIN, :] = jnp.zeros((MAX_WIN, d_model), F32)

    @pl.when((bi == 0) & (i == 0))
    def _():
        run_ref[...] = jnp.zeros_like(run_ref)

    x = h_ref[...]
    ext_ref[MAX_WIN:MAX_WIN + tile, :] = x
    pos = i * tile + lax.broadcasted_iota(jnp.int32, (tile, 1), 0)
    ys = []
    for g, w in enumerate(POOL_WINDOWS):
        lo, hi = g * gdim, (g + 1) * gdim
        xg = x[:, lo:hi]
        s = xg
        for j in range(1, w):
            s = s + ext_ref[MAX_WIN - j:MAX_WIN - j + tile, lo:hi]
        cnt = jnp.minimum(pos + 1, w).astype(F32)
        u = s / cnt - xg
        ys.append(jnp.dot(u.astype(BF16), pw_ref[g].astype(BF16), preferred_element_type=F32))
    y = jnp.concatenate(ys, axis=-1) * ps_ref[...]
    hn = _layer_norm(alpha * x + y, g_ref[...], b_ref[...])
    h1_ref[...] = hn
    ext_ref[0:MAX_WIN, :] = x[tile - MAX_WIN:tile, :]
    _route(hn, rw_ref, rb_ref, run_ref, ti_ref, gate_ref, rank_ref, cnt_ref)


def _route_out_shapes(tp, n_exp):
    return [jax.ShapeDtypeStruct((tp, TOP_K), jnp.int32),
            jax.ShapeDtypeStruct((tp, TOP_K), F32),
            jax.ShapeDtypeStruct((tp, TOP_K), jnp.int32),
            jax.ShapeDtypeStruct((1, n_exp), F32)]


def _route_out_specs(nt, tile, n_exp):
    row = lambda b, i: (b * nt + i, 0)
    return [pl.BlockSpec((tile, TOP_K), row),
            pl.BlockSpec((tile, TOP_K), row),
            pl.BlockSpec((tile, TOP_K), row),
            pl.BlockSpec((1, n_exp), lambda b, i: (0, 0))]


def _const_spec(shape):
    return pl.BlockSpec(shape, lambda b, i: (0,) * len(shape))


def _pool_call(h, pool_w, pool_scale, ln_g, ln_b, router_w, router_b, *, bsz, lp, alpha):
    tp, d = h.shape
    tile = SEQ_TILE
    nt = lp // tile
    n_exp = router_w.shape[1]
    groups, gdim, _ = pool_w.shape
    row = lambda b, i: (b * nt + i, 0)
    return pl.pallas_call(
        functools.partial(_pool_kernel, tile=tile, alpha=alpha),
        grid=(bsz, nt),
        in_specs=[pl.BlockSpec((tile, d), row),
                  _const_spec((groups, gdim, gdim)),
                  _const_spec((1, d)), _const_spec((1, d)), _const_spec((1, d)),
                  _const_spec((d, n_exp)), _const_spec((1, n_exp))],
        out_specs=[pl.BlockSpec((tile, d), row)] + _route_out_specs(nt, tile, n_exp),
        out_shape=[jax.ShapeDtypeStruct((tp, d), F32)] + _route_out_shapes(tp, n_exp),
        scratch_shapes=[pltpu.VMEM((MAX_WIN + tile, d), F32), pltpu.VMEM((1, n_exp), F32)],
        compiler_params=_params(("arbitrary", "arbitrary")),
    )(h, pool_w, pool_scale.reshape(1, d), ln_g.reshape(1, d), ln_b.reshape(1, d),
      router_w, router_b.reshape(1, n_exp))


def _plan(counts, top_i, rank, n_blocks):
    n_exp = counts.shape[1]
    cnt = counts.reshape(n_exp).astype(jnp.int32)
    padded = (cnt + EXPERT_ROWS - 1) // EXPERT_ROWS * EXPERT_ROWS
    pad_ends = jnp.cumsum(padded)
    pad_starts = pad_ends - padded
    hot = top_i[..., None] == jnp.arange(n_exp, dtype=jnp.int32)
    dest = jnp.sum(jnp.where(hot, pad_starts, 0), axis=-1) + rank
    starts = jnp.arange(n_blocks, dtype=jnp.int32) * EXPERT_ROWS
    block_e = jnp.sum((starts[:, None] >= pad_ends[None, :]).astype(jnp.int32), axis=1)
    block_e = jnp.minimum(block_e, n_exp - 1)
    n_used = (pad_ends[-1] // EXPERT_ROWS).reshape(1)
    return dest.reshape(-1).astype(jnp.int32), block_e, n_used


def _dispatch_kernel(dest_ref, h_ref, xs_in_ref, xs_ref, sem, *, tile):
    del xs_in_ref

    def row_copy(r, k, d):
        return pltpu.make_async_copy(h_ref.at[pl.ds(r, 1), :], xs_ref.at[pl.ds(d, 1), :], sem)

    def issue(r, carry):
        for k in range(TOP_K):
            row_copy(r, k, dest_ref[r * TOP_K + k]).start()
        return carry

    lax.fori_loop(0, tile, issue, 0)

    def drain(r, carry):
        for k in range(TOP_K):
            row_copy(0, k, 0).wait()
        return carry

    lax.fori_loop(0, tile, drain, 0)


def _dispatch_call(dest, h, n_rows):
    tp, d = h.shape
    tile = ROUTE_TILE
    xs0 = jnp.zeros((n_rows, d), F32)
    return pl.pallas_call(
        functools.partial(_dispatch_kernel, tile=tile),
        grid=(tp // tile,),
        in_specs=[pl.BlockSpec((tile * TOP_K,), lambda i: (i,), memory_space=pltpu.SMEM),
                  pl.BlockSpec((tile, d), lambda i: (i, 0)),
                  pl.BlockSpec(memory_space=pl.ANY)],
        out_specs=pl.BlockSpec(memory_space=pl.ANY),
        out_shape=jax.ShapeDtypeStruct((n_rows, d), F32),
        scratch_shapes=[pltpu.SemaphoreType.DMA(())],
        input_output_aliases={2: 0},
        compiler_params=_params(("arbitrary",)),
    )(dest, h, xs0)


def _ffn_kernel(be_ref, nu_ref, x_ref, w1_ref, b1_ref, w2_ref, b2_ref, y_ref, w1b_ref, w2b_ref):
    i = pl.program_id(0)
    f = w2_ref.shape[1]
    active = i < nu_ref[0]
    changed = (i == 0) | (be_ref[i] != be_ref[jnp.maximum(i - 1, 0)])

    @pl.when(active & changed)
    def _():
        w1b_ref[...] = w1_ref[0].astype(BF16)
        w2b_ref[...] = w2_ref[0].astype(BF16)

    @pl.when(active)
    def _():
        x = x_ref[...].astype(BF16)
        hid = jnp.dot(x, w1b_ref[...], preferred_element_type=F32) + b1_ref[0]
        gate = jnp.minimum(hid[:, :f], SWIGLU_LIMIT)
        up = jnp.clip(hid[:, f:], -SWIGLU_LIMIT, SWIGLU_LIMIT)
        act = gate * jax.nn.sigmoid(SWIGLU_ALPHA * gate) * (up + 1.0)
        y_ref[...] = jnp.dot(act.astype(BF16), w2b_ref[...], preferred_element_type=F32) + b2_ref[0]

    @pl.when(jnp.logical_not(active))
    def _():
        y_ref[...] = jnp.zeros_like(y_ref)


def _ffn_call(block_e, n_used, xs, w1, b1, w2, b2):
    n_rows, d = xs.shape
    n_exp, _, f2 = w1.shape
    f = w2.shape[1]
    nb = n_rows // EXPERT_ROWS

    def blk(i, be, nu):
        return jnp.minimum(i, nu[0] - 1)

    grid_spec = pltpu.PrefetchScalarGridSpec(
        num_scalar_prefetch=2,
        grid=(nb,),
        in_specs=[pl.BlockSpec((EXPERT_ROWS, d), lambda i, be, nu: (blk(i, be, nu), 0)),
                  pl.BlockSpec((1, d, f2), lambda i, be, nu: (be[blk(i, be, nu)], 0, 0)),
                  pl.BlockSpec((1, 1, f2), lambda i, be, nu: (be[blk(i, be, nu)], 0, 0)),
                  pl.BlockSpec((1, f, d), lambda i, be, nu: (be[blk(i, be, nu)], 0, 0)),
                  pl.BlockSpec((1, 1, d), lambda i, be, nu: (be[blk(i, be, nu)], 0, 0))],
        out_specs=pl.BlockSpec((EXPERT_ROWS, d), lambda i, be, nu: (i, 0)),
        scratch_shapes=[pltpu.VMEM((d, f2), BF16), pltpu.VMEM((f, d), BF16)],
    )
    return pl.pallas_call(
        _ffn_kernel,
        grid_spec=grid_spec,
        out_shape=jax.ShapeDtypeStruct((n_rows, d), F32),
        compiler_params=_params(("arbitrary",)),
    )(block_e, n_used, xs, w1, b1.reshape(n_exp, 1, f2), w2, b2.reshape(n_exp, 1, d))


def _combine_kernel(dest_ref, gate_ref, h_ref, g_ref, b_ref, y_ref, out_ref, buf_ref, sem, *, tile, alpha):
    def row_copy(r, k, d):
        return pltpu.make_async_copy(y_ref.at[pl.ds(d, 1), :], buf_ref.at[k, pl.ds(r, 1), :], sem)

    def issue(r, carry):
        for k in range(TOP_K):
            row_copy(r, k, dest_ref[r * TOP_K + k]).start()
        return carry

    lax.fori_loop(0, tile, issue, 0)

    def drain(r, carry):
        for k in range(TOP_K):
            row_copy(0, k, 0).wait()
        return carry

    lax.fori_loop(0, tile, drain, 0)

    gates = gate_ref[...]
    ffn = gates[:, 0:1] * buf_ref[0]
    for k in range(1, TOP_K):
        ffn = ffn + gates[:, k:k + 1] * buf_ref[k]
    out_ref[...] = _layer_norm(alpha * h_ref[...] + ffn, g_ref[...], b_ref[...])


def _combine_call(dest, gates, h, y, ln_g, ln_b, *, alpha):
    tp, d = h.shape
    tile = ROUTE_TILE
    return pl.pallas_call(
        functools.partial(_combine_kernel, tile=tile, alpha=alpha),
        grid=(tp // tile,),
        in_specs=[pl.BlockSpec((tile * TOP_K,), lambda i: (i,), memory_space=pltpu.SMEM),
                  pl.BlockSpec((tile, TOP_K), lambda i: (i, 0)),
                  pl.BlockSpec((tile, d), lambda i: (i, 0)),
                  pl.BlockSpec((1, d), lambda i: (0, 0)),
                  pl.BlockSpec((1, d), lambda i: (0, 0)),
                  pl.BlockSpec(memory_space=pl.ANY)],
        out_specs=pl.BlockSpec((tile, d), lambda i: (i, 0)),
        out_shape=jax.ShapeDtypeStruct((tp, d), F32),
        scratch_shapes=[pltpu.VMEM((TOP_K, tile, d), F32), pltpu.SemaphoreType.DMA(())],
        compiler_params=_params(("arbitrary",)),
    )(dest, gates, h, ln_g.reshape(1, d), ln_b.reshape(1, d), y)


def _moe(h, top_i, gates, rank, counts, w1, b1, w2, b2, ln_g, ln_b, *, alpha):
    tp = h.shape[0]
    n_exp = w1.shape[0]
    n_blocks = tp * TOP_K // EXPERT_ROWS + n_exp
    dest, block_e, n_used = _plan(counts, top_i, rank, n_blocks)
    xs = _dispatch_call(dest, h, n_blocks * EXPERT_ROWS)
    y = _ffn_call(block_e, n_used, xs, w1, b1, w2, b2)
    return _combine_call(dest, gates, h, y, ln_g, ln_b, alpha=alpha)


def _log_sigmoid(x):
    return jnp.minimum(x, 0.0) - jnp.log(1.0 + jnp.exp(-jnp.abs(x)))


def _split3(c):
    hi = c.astype(BF16)
    r1 = c - hi.astype(F32)
    mid = r1.astype(BF16)
    lo = (r1 - mid.astype(F32)).astype(BF16)
    return hi, mid, lo


def _proj_kernel(h_ref, wqt_ref, wk_ref, wvt_ref, wf_ref, wft_ref, bf_ref, bft_ref,
                 selq_ref, selk_ref, oneq_ref, onek_ref, onev_ref,
                 qt_ref, kx_ref, vt_ref, carry_ref, carryt_ref, *, tile, n_heads):
    i = pl.program_id(1)

    @pl.when(i == 0)
    def _():
        carry_ref[...] = jnp.zeros_like(carry_ref)
        carryt_ref[...] = jnp.zeros_like(carryt_ref)

    x = h_ref[...]
    xb = x.astype(BF16)
    nt_dims = (((1,), (1,)), ((), ()))
    lf = _log_sigmoid(jnp.dot(x, wf_ref[...], precision=HIGHEST, preferred_element_type=F32) + bf_ref[...])
    lft = _log_sigmoid(lax.dot_general(wft_ref[...], x, nt_dims, precision=HIGHEST,
                                       preferred_element_type=F32) + bft_ref[...])
    r = lax.broadcasted_iota(jnp.int32, (tile, tile), 0)
    c = lax.broadcasted_iota(jnp.int32, (tile, tile), 1)
    cs = jnp.dot((c <= r).astype(F32), lf, precision=HIGHEST, preferred_element_type=F32) + carry_ref[...]
    cst = jnp.dot(lft, (r <= c).astype(F32), precision=HIGHEST, preferred_element_type=F32) + carryt_ref[...]
    carry_ref[...] = cs[tile - 1:tile, :]
    carryt_ref[...] = cst[:, tile - 1:tile]

    qt = lax.dot_general(wqt_ref[...], xb, nt_dims, preferred_element_type=F32) + oneq_ref[...]
    for p, part in enumerate(_split3(cst * LOG2E)):
        qt = qt + jnp.dot(selq_ref[p], part, preferred_element_type=F32)
    kx = jnp.dot(xb, wk_ref[...], preferred_element_type=F32) + onek_ref[...]
    for p, part in enumerate(_split3(-cs * LOG2E)):
        kx = kx + jnp.dot(part, selk_ref[p], preferred_element_type=F32)
    vt = lax.dot_general(wvt_ref[...], xb, nt_dims, preferred_element_type=F32) + onev_ref[...]
    for hd in range(n_heads):
        qt_ref[0, hd, 0] = qt[hd * LANES:(hd + 1) * LANES, :].astype(BF16)
        kx_ref[0, hd, 0] = kx[:, hd * LANES:(hd + 1) * LANES].astype(BF16)
        vt_ref[0, hd, 0] = vt[hd * LANES:(hd + 1) * LANES, :].astype(BF16)


def _attn_weights(w_in, b_f, n_heads, head_dim):
    d = w_in.shape[0]
    scale = head_dim ** -0.5 * LOG2E
    hw = n_heads * LANES

    def pad_heads(w):
        w = w.reshape(d, n_heads, head_dim)
        return jnp.pad(w, ((0, 0), (0, 0), (0, LANES - head_dim))).reshape(d, hw)

    wqt = pad_heads(w_in[:, :d] * scale).T.astype(BF16)
    wk = pad_heads(w_in[:, d:2 * d]).astype(BF16)
    wvt = pad_heads(w_in[:, 2 * d:3 * d]).T.astype(BF16)
    wf = w_in[:, 3 * d:]
    selq = np.zeros((3, hw, n_heads), np.float32)
    selk = np.zeros((3, n_heads, hw), np.float32)
    oneq = np.zeros((hw, 1), np.float32)
    onek = np.zeros((1, hw), np.float32)
    onev = np.zeros((hw, 1), np.float32)
    for h in range(n_heads):
        base = h * LANES + head_dim
        for p in range(3):
            selq[p, base + p, h] = 1.0
            selk[p, h, base + 3 + p] = 1.0
            oneq[base + 3 + p, 0] = 1.0
            onek[0, base + p] = 1.0
        onev[base, 0] = 1.0
    return (wqt, wk, wvt, wf, wf.T, b_f.reshape(1, n_heads), b_f.reshape(n_heads, 1),
            jnp.asarray(selq, BF16), jnp.asarray(selk, BF16),
            jnp.asarray(oneq), jnp.asarray(onek), jnp.asarray(onev))


def _proj_call(h, weights, *, bsz, lp, n_heads):
    tp, d = h.shape
    tile = SEQ_TILE
    nt = lp // tile
    hw = n_heads * LANES
    row = lambda b, i: (b * nt + i, 0)
    in_specs = [pl.BlockSpec((tile, d), row)] + [_const_spec(w.shape) for w in weights]
    t_spec = pl.BlockSpec((1, n_heads, 1, LANES, tile), lambda b, i: (b, 0, i, 0, 0))
    k_spec = pl.BlockSpec((1, n_heads, 1, tile, LANES), lambda b, i: (b, 0, i, 0, 0))
    return pl.pallas_call(
        functools.partial(_proj_kernel, tile=tile, n_heads=n_heads),
        grid=(bsz, nt),
        in_specs=in_specs,
        out_specs=[t_spec, k_spec, t_spec],
        out_shape=[jax.ShapeDtypeStruct((bsz, n_heads, nt, LANES, tile), BF16),
                   jax.ShapeDtypeStruct((bsz, n_heads, nt, tile, LANES), BF16),
                   jax.ShapeDtypeStruct((bsz, n_heads, nt, LANES, tile), BF16)],
        scratch_shapes=[pltpu.VMEM((1, n_heads), F32), pltpu.VMEM((n_heads, 1), F32)],
        compiler_params=_params(("arbitrary", "arbitrary")),
    )(h, *weights)


def _attn_kernel(qt_ref, kx_ref, vt_ref, o_ref, s0_ref, s1_ref, s2_ref, p0_ref, p1_ref, p2_ref,
                 a0_ref, a1_ref, a2_ref, c0_ref, c1_ref, c2_ref, m_ref, acc_ref, *, nt, tile, head_dim):
    s_refs, p_refs, a_refs = (s0_ref, s1_ref, s2_ref), (p0_ref, p1_ref, p2_ref), (a0_ref, a1_ref, a2_ref)
    c_refs = (c0_ref, c1_ref, c2_ref)
    key = lax.broadcasted_iota(jnp.int32, (tile, tile), 0)
    qry = lax.broadcasted_iota(jnp.int32, (tile, tile), 1)

    def logits(slot, qi, kj):
        s = jnp.dot(kx_ref[0, 0, kj], qt_ref[0, 0, qi], preferred_element_type=F32)
        s_refs[slot][...] = s
        c_refs[slot][...] = jnp.max(s, axis=0, keepdims=True)

    def value_update(slot, kj, out_qi=None):
        acc = a_refs[slot][...] * acc_ref[...] + jnp.dot(
            vt_ref[0, 0, kj], p_refs[slot][...], preferred_element_type=F32)
        acc_ref[...] = acc
        if out_qi is not None:
            o_ref[0, 0, out_qi] = (acc[:head_dim, :] * (1.0 / acc[head_dim:head_dim + 1, :])).astype(BF16)

    def substep(slot, qi, kj, *, diag, write_out):
        if diag:
            nqi, nkj = qi + 1, jnp.int32(1)
        else:
            stay = qi - kj >= 2
            nqi, nkj = jnp.where(stay, qi, qi + 1), jnp.where(stay, kj + 2, 0)
        logits((slot + 2) % ATTN_DEPTH, jnp.minimum(nqi, nt - 1), nkj)
        pkj = jnp.where(kj >= 2, kj - 2, jnp.maximum(qi - 1 - (kj == 0).astype(jnp.int32), 0))
        out_qi = jnp.where(kj == 1, qi - 1, qi) if write_out else None
        value_update((slot + 1) % ATTN_DEPTH, pkj, out_qi)
        m_old = m_ref[...]
        if diag:
            s = jnp.where(key <= qry, s_refs[slot][...], MASK_VALUE)
            m_new = jnp.maximum(m_old, jnp.max(s, axis=0, keepdims=True))
            p_refs[slot][...] = jnp.exp2(s - m_new).astype(BF16)
            m_ref[...] = jnp.full_like(m_old, MASK_VALUE)
        else:
            m_new = jnp.maximum(m_old, c_refs[slot][...])
            p_refs[slot][...] = jnp.exp2(s_refs[slot][...] - m_new).astype(BF16)
            m_ref[...] = m_new
        a_refs[slot][...] = jnp.exp2(m_old - m_new)

    def q_tile(slot, qi, n_loops, rem):
        def trip(i, carry):
            for j in range(ATTN_DEPTH):
                substep((slot + j) % ATTN_DEPTH, qi, ATTN_DEPTH * i + j, diag=False, write_out=j == 1)
            return carry

        lax.fori_loop(0, n_loops, trip, 0)
        for j in range(rem):
            substep((slot + j) % ATTN_DEPTH, qi, ATTN_DEPTH * n_loops + j, diag=False, write_out=j == 1)
        substep((slot + rem) % ATTN_DEPTH, qi, qi, diag=True, write_out=rem == 1)
        return (slot + rem + 1) % ATTN_DEPTH

    m_ref[...] = jnp.full_like(m_ref, MASK_VALUE)
    acc_ref[...] = jnp.ones_like(acc_ref)
    for slot in range(1, ATTN_DEPTH):
        p_refs[slot][...] = jnp.zeros_like(p_refs[slot])
        a_refs[slot][...] = jnp.ones_like(a_refs[slot])
    logits(0, 0, 0)
    logits(1, 1, 0)

    def group(g, carry):
        slot = 0
        for r in range(ATTN_DEPTH):
            slot = q_tile(slot, ATTN_DEPTH * g + r, g, r)
        assert slot == 0
        return carry

    lax.fori_loop(0, nt // ATTN_DEPTH, group, 0)
    slot = 0
    for qi in range(nt // ATTN_DEPTH * ATTN_DEPTH, nt):
        slot = q_tile(slot, jnp.int32(qi), jnp.int32(qi // ATTN_DEPTH), qi % ATTN_DEPTH)
    value_update((slot + 1) % ATTN_DEPTH, nt - 2)
    value_update((slot + 2) % ATTN_DEPTH, nt - 1, nt - 1)


def _attn_call(qt, kx, vt, *, head_dim):
    bsz, n_heads, nt, _, tile = qt.shape
    return pl.pallas_call(
        functools.partial(_attn_kernel, nt=nt, tile=tile, head_dim=head_dim),
        grid=(bsz, n_heads),
        in_specs=[pl.BlockSpec((1, 1, nt, LANES, tile), lambda b, h: (b, h, 0, 0, 0)),
                  pl.BlockSpec((1, 1, nt, tile, LANES), lambda b, h: (b, h, 0, 0, 0)),
                  pl.BlockSpec((1, 1, nt, LANES, tile), lambda b, h: (b, h, 0, 0, 0))],
        out_specs=pl.BlockSpec((1, 1, nt, head_dim, tile), lambda b, h: (b, h, 0, 0, 0)),
        out_shape=jax.ShapeDtypeStruct((bsz, n_heads, nt, head_dim, tile), BF16),
        scratch_shapes=([pltpu.VMEM((tile, tile), F32)] * ATTN_DEPTH + [pltpu.VMEM((tile, tile), BF16)] * ATTN_DEPTH
                        + [pltpu.VMEM((1, tile), F32)] * (2 * ATTN_DEPTH)
                        + [pltpu.VMEM((1, tile), F32), pltpu.VMEM((LANES, tile), F32)]),
        compiler_params=_params(("arbitrary", "arbitrary")),
    )(qt, kx, vt)


def _oproj_kernel(o_ref, wo_ref, h_ref, g_ref, b_ref, rw_ref, rb_ref,
                  hn_ref, ti_ref, gate_ref, rank_ref, cnt_ref, run_ref, *, alpha):
    bi = pl.program_id(0)
    i = pl.program_id(1)

    @pl.when((bi == 0) & (i == 0))
    def _():
        run_ref[...] = jnp.zeros_like(run_ref)

    n_heads, _, head_dim, tile = o_ref.shape[1:]
    o_t = o_ref[0].reshape(n_heads * head_dim, tile)
    att = lax.dot_general(o_t, wo_ref[...], (((0,), (0,)), ((), ())), preferred_element_type=F32)
    hn = _layer_norm(alpha * h_ref[...] + att, g_ref[...], b_ref[...])
    hn_ref[...] = hn
    _route(hn, rw_ref, rb_ref, run_ref, ti_ref, gate_ref, rank_ref, cnt_ref)


def _oproj_call(o, w_out, h, ln_g, ln_b, router_w, router_b, *, bsz, lp, alpha):
    tp, d = h.shape
    tile = SEQ_TILE
    nt = lp // tile
    n_exp = router_w.shape[1]
    row = lambda b, i: (b * nt + i, 0)
    return pl.pallas_call(
        functools.partial(_oproj_kernel, alpha=alpha),
        grid=(bsz, nt),
        in_specs=[pl.BlockSpec((1,) + o.shape[1:2] + (1,) + o.shape[3:], lambda b, i: (b, 0, i, 0, 0)),
                  _const_spec((d, d)),
                  pl.BlockSpec((tile, d), row),
                  _const_spec((1, d)), _const_spec((1, d)),
                  _const_spec((d, n_exp)), _const_spec((1, n_exp))],
        out_specs=[pl.BlockSpec((tile, d), row)] + _route_out_specs(nt, tile, n_exp),
        out_shape=[jax.ShapeDtypeStruct((tp, d), F32)] + _route_out_shapes(tp, n_exp),
        scratch_shapes=[pltpu.VMEM((1, n_exp), F32)],
        compiler_params=_params(("arbitrary", "arbitrary")),
    )(o, w_out.astype(BF16), h, ln_g.reshape(1, d), ln_b.reshape(1, d),
      router_w, router_b.reshape(1, n_exp))


def kernel(x, meta_tokens, pool_w, pool_scale, attn_w_in, attn_b_f, attn_w_out,
           ln_g, ln_b, router_w, router_b, w1, b1, w2, b2):
    bsz, seq, d = x.shape
    n_meta = meta_tokens.shape[0]
    depth = ln_g.shape[0]
    n_heads = attn_b_f.shape[-1]
    head_dim = d // n_heads
    alpha = float((2 * depth) ** 0.25)
    length = n_meta + seq
    lp = -(-length // SEQ_TILE) * SEQ_TILE
    assert d % (len(POOL_WINDOWS) * LANES) == 0 and head_dim + 6 <= LANES and depth == 2

    meta = jnp.broadcast_to(meta_tokens[None], (bsz, n_meta, d))
    h = jnp.concatenate([meta, x, jnp.zeros((bsz, lp - length, d), x.dtype)], axis=1)
    h = h.reshape(bsz * lp, d)

    h, top_i, gates, rank, counts = _pool_call(
        h, pool_w[0], pool_scale[0], ln_g[0, 0], ln_b[0, 0], router_w[0], router_b[0],
        bsz=bsz, lp=lp, alpha=alpha)
    h = _moe(h, top_i, gates, rank, counts, w1[0], b1[0], w2[0], b2[0], ln_g[0, 1], ln_b[0, 1], alpha=alpha)

    weights = _attn_weights(attn_w_in[0], attn_b_f[0], n_heads, head_dim)
    qt, kx, vt = _proj_call(h, weights, bsz=bsz, lp=lp, n_heads=n_heads)
    o = _attn_call(qt, kx, vt, head_dim=head_dim)
    h, top_i, gates, rank, counts = _oproj_call(
        o, attn_w_out[0], h, ln_g[1, 0], ln_b[1, 0], router_w[1], router_b[1],
        bsz=bsz, lp=lp, alpha=alpha)
    h = _moe(h, top_i, gates, rank, counts, w1[1], b1[1], w2[1], b2[1], ln_g[1, 1], ln_b[1, 1], alpha=alpha)

    return h.reshape(bsz, lp, d)[:, n_meta:length]
```

```python
import functools

import numpy as np
import jax
import jax.numpy as jnp
from jax import lax
from jax.experimental import pallas as pl
from jax.experimental.pallas import tpu as pltpu

POOL_WINDOWS = (2, 4, 8, 16)
MAX_WIN = max(POOL_WINDOWS)
TOP_K = 4
SWIGLU_LIMIT = 7.0
SWIGLU_ALPHA = 1.702
LN_EPS = 1e-5
MASK_VALUE = -1e30

LANES = 128
SEQ_TILE = 512
ROUTE_TILE = 256
EXPERT_ROWS = 512
VMEM_LIMIT = 56 * 1024 * 1024
LOG2E = 1.4426950408889634
ATTN_DEPTH = 3

F32 = jnp.float32
BF16 = jnp.bfloat16
HIGHEST = lax.Precision.HIGHEST


def _params(sem, vmem=VMEM_LIMIT):
    return pltpu.CompilerParams(dimension_semantics=sem, vmem_limit_bytes=vmem)


def _layer_norm(z, g, b):
    mu = jnp.mean(z, axis=-1, keepdims=True)
    d = z - mu
    var = jnp.mean(d * d, axis=-1, keepdims=True)
    return d * lax.rsqrt(var + LN_EPS) * g + b


def _route(hn, rw_ref, rb_ref, run_ref, ti_ref, gate_ref, rank_ref, cnt_ref):
    rows = hn.shape[0]
    logits = jnp.dot(hn, rw_ref[...], precision=HIGHEST, preferred_element_type=F32) + rb_ref[...]
    n_exp = logits.shape[1]
    lane = lax.broadcasted_iota(jnp.int32, (rows, n_exp), 1)
    cur = logits
    vals, idxs, hots = [], [], []
    for _ in range(TOP_K):
        m = jnp.max(cur, axis=-1, keepdims=True)
        idx = jnp.min(jnp.where(cur == m, lane, n_exp), axis=-1, keepdims=True)
        hot = lane == idx
        vals.append(m)
        idxs.append(idx)
        hots.append(hot)
        cur = jnp.where(hot, -jnp.inf, cur)
    exps = [jnp.exp(v - vals[0]) for v in vals]
    denom = exps[0]
    for e in exps[1:]:
        denom = denom + e
    gates = [e / denom for e in exps]
    sel = hots[0].astype(F32)
    for hot in hots[1:]:
        sel = sel + hot.astype(F32)
    r = lax.broadcasted_iota(jnp.int32, (rows, rows), 0)
    c = lax.broadcasted_iota(jnp.int32, (rows, rows), 1)
    tri = (c < r).astype(BF16)
    base = run_ref[...] + jnp.dot(tri, sel.astype(BF16), preferred_element_type=F32)
    ranks = [jnp.sum(jnp.where(hot, base, 0.0), axis=-1, keepdims=True) for hot in hots]
    run_ref[...] = run_ref[...] + jnp.sum(sel, axis=0, keepdims=True)
    cnt_ref[...] = run_ref[...]
    lane_k = lax.broadcasted_iota(jnp.int32, (rows, TOP_K), 1)

    def pack(cols):
        out = jnp.broadcast_to(cols[TOP_K - 1], (rows, TOP_K))
        for k in range(TOP_K - 2, -1, -1):
            out = jnp.where(lane_k == k, cols[k], out)
        return out

    ti_ref[...] = pack(idxs)
    gate_ref[...] = pack(gates)
    rank_ref[...] = pack(ranks).astype(jnp.int32)


def _pool_kernel(h_ref, pw_ref, ps_ref, g_ref, b_ref, rw_ref, rb_ref,
                 h1_ref, ti_ref, gate_ref, rank_ref, cnt_ref,
                 ext_ref, run_ref, *, tile, alpha):
    bi = pl.program_id(0)
    i = pl.program_id(1)
    d_model = h_ref.shape[1]
    gdim = d_model // len(POOL_WINDOWS)

    @pl.when(i == 0)
    def _():
        ext_ref[0:MAX_WIN, :] = jnp.zeros((MAX_WIN, d_model), F32)

    @pl.when((bi == 0) & (i == 0))
    def _():
        run_ref[...] = jnp.zeros_like(run_ref)

    x = h_ref[...]
    ext_ref[MAX_WIN:MAX_WIN + tile, :] = x
    pos = i * tile + lax.broadcasted_iota(jnp.int32, (tile, 1), 0)
    ys = []
    for g, w in enumerate(POOL_WINDOWS):
        lo, hi = g * gdim, (g + 1) * gdim
        xg = x[:, lo:hi]
        s = xg
        for j in range(1, w):
            s = s + ext_ref[MAX_WIN - j:MAX_WIN - j + tile, lo:hi]
        cnt = jnp.minimum(pos + 1, w).astype(F32)
        u = s / cnt - xg
        ys.append(jnp.dot(u.astype(BF16), pw_ref[g].astype(BF16), preferred_element_type=F32))
    y = jnp.concatenate(ys, axis=-1) * ps_ref[...]
    hn = _layer_norm(alpha * x + y, g_ref[...], b_ref[...])
    h1_ref[...] = hn
    ext_ref[0:MAX_WIN, :] = x[tile - MAX_WIN:tile, :]
    _route(hn, rw_ref, rb_ref, run_ref, ti_ref, gate_ref, rank_ref, cnt_ref)


def _route_out_shapes(tp, n_exp):
    return [jax.ShapeDtypeStruct((tp, TOP_K), jnp.int32),
            jax.ShapeDtypeStruct((tp, TOP_K), F32),
            jax.ShapeDtypeStruct((tp, TOP_K), jnp.int32),
            jax.ShapeDtypeStruct((1, n_exp), F32)]


def _route_out_specs(nt, tile, n_exp):
    row = lambda b, i: (b * nt + i, 0)
    return [pl.BlockSpec((tile, TOP_K), row),
            pl.BlockSpec((tile, TOP_K), row),
            pl.BlockSpec((tile, TOP_K), row),
            pl.BlockSpec((1, n_exp), lambda b, i: (0, 0))]


def _const_spec(shape):
    return pl.BlockSpec(shape, lambda b, i: (0,) * len(shape))


def _pool_call(h, pool_w, pool_scale, ln_g, ln_b, router_w, router_b, *, bsz, lp, alpha):
    tp, d = h.shape
    tile = SEQ_TILE
    nt = lp // tile
    n_exp = router_w.shape[1]
    groups, gdim, _ = pool_w.shape
    row = lambda b, i: (b * nt + i, 0)
    return pl.pallas_call(
        functools.partial(_pool_kernel, tile=tile, alpha=alpha),
        grid=(bsz, nt),
        in_specs=[pl.BlockSpec((tile, d), row),
                  _const_spec((groups, gdim, gdim)),
                  _const_spec((1, d)), _const_spec((1, d)), _const_spec((1, d)),
                  _const_spec((d, n_exp)), _const_spec((1, n_exp))],
        out_specs=[pl.BlockSpec((tile, d), row)] + _route_out_specs(nt, tile, n_exp),
        out_shape=[jax.ShapeDtypeStruct((tp, d), F32)] + _route_out_shapes(tp, n_exp),
        scratch_shapes=[pltpu.VMEM((MAX_WIN + tile, d), F32), pltpu.VMEM((1, n_exp), F32)],
        compiler_params=_params(("arbitrary", "arbitrary")),
    )(h, pool_w, pool_scale.reshape(1, d), ln_g.reshape(1, d), ln_b.reshape(1, d),
      router_w, router_b.reshape(1, n_exp))


def _plan(counts, top_i, rank, n_blocks):
    tp = top_i.shape[0]
    n_exp = counts.shape[1]
    n_rows = n_blocks * EXPERT_ROWS
    cnt = counts.reshape(n_exp).astype(jnp.int32)
    padded = (cnt + EXPERT_ROWS - 1) // EXPERT_ROWS * EXPERT_ROWS
    pad_ends = jnp.cumsum(padded)
    pad_starts = pad_ends - padded
    hot = top_i[..., None] == jnp.arange(n_exp, dtype=jnp.int32)
    dest = (jnp.sum(jnp.where(hot, pad_starts, 0), axis=-1) + rank).reshape(-1)
    starts = jnp.arange(n_blocks, dtype=jnp.int32) * EXPERT_ROWS
    block_e = jnp.sum((starts[:, None] >= pad_ends[None, :]).astype(jnp.int32), axis=1)
    block_e = jnp.minimum(block_e, n_exp - 1)
    n_used = (pad_ends[-1] // EXPERT_ROWS).reshape(1)

    copy = jnp.arange(tp * TOP_K, dtype=jnp.int32)
    row_copy = jnp.full((n_rows,), -1, jnp.int32).at[dest].set(copy, unique_indices=True)
    valid = row_copy >= 0
    tok, k = row_copy // TOP_K, row_copy % TOP_K
    scrap = tp * TOP_K + jnp.cumsum(jnp.logical_not(valid).astype(jnp.int32)) - 1
    row_tok = jnp.where(valid, tok, 0)
    row_dst = jnp.where(valid, k * tp + tok, scrap)
    first = n_rows + jnp.arange(EXPERT_ROWS, dtype=jnp.int32) % (EXPERT_ROWS // 2)
    dst_ext = jnp.concatenate([first, row_dst])
    return (block_e, n_used, row_tok.reshape(n_blocks, 1, EXPERT_ROWS),
            dst_ext.reshape(n_blocks + 1, 1, EXPERT_ROWS))


def _ffn_kernel(be_ref, nu_ref, tok_ref, tokn_ref, dstp_ref, dst_ref, h_ref, w1_ref, b1_ref, w2_ref, b2_ref,
                y_ref, xa_ref, xb_ref, ya_ref, yb_ref, w1b_ref, w2b_ref, gsem, ssem, *, n_blocks):
    i = pl.program_id(0)
    half = xa_ref.shape[0]
    f = w2_ref.shape[1]

    def gather(idx_ref, base, buf_ref, sem):
        for r in range(half):
            pltpu.make_async_copy(h_ref.at[pl.ds(idx_ref[0, 0, base + r], 1), :],
                                  buf_ref.at[pl.ds(r, 1), :], sem).start()

    def scatter(idx_ref, base, buf_ref, sem):
        for r in range(half):
            pltpu.make_async_copy(buf_ref.at[pl.ds(r, 1), :],
                                  y_ref.at[pl.ds(idx_ref[0, 0, base + r], 1), :], sem).start()

    def wait_gather(buf_ref, sem):
        pltpu.make_async_copy(h_ref.at[pl.ds(0, half), :], buf_ref, sem).wait()

    def wait_scatter(buf_ref, sem):
        pltpu.make_async_copy(buf_ref, y_ref.at[pl.ds(0, half), :], sem).wait()

    def ffn(x_ref, out_ref):
        hid = jnp.dot(x_ref[...].astype(BF16), w1b_ref[...], preferred_element_type=F32) + b1_ref[0]
        gate = jnp.minimum(hid[:, :f], SWIGLU_LIMIT)
        up = jnp.clip(hid[:, f:], -SWIGLU_LIMIT, SWIGLU_LIMIT)
        act = gate * jax.nn.sigmoid(SWIGLU_ALPHA * gate) * (up + 1.0)
        out_ref[...] = jnp.dot(act.astype(BF16), w2b_ref[...], preferred_element_type=F32) + b2_ref[0]

    @pl.when(i == 0)
    def _():
        yb_ref[...] = jnp.zeros_like(yb_ref)
        gather(tok_ref, 0, xa_ref, gsem.at[0])

    @pl.when((i == 0) | (be_ref[i] != be_ref[jnp.maximum(i - 1, 0)]))
    def _():
        w1b_ref[...] = w1_ref[0].astype(BF16)
        w2b_ref[...] = w2_ref[0].astype(BF16)

    @pl.when(i >= 1)
    def _():
        wait_scatter(ya_ref, ssem.at[0])

    wait_gather(xa_ref, gsem.at[0])
    gather(tok_ref, half, xb_ref, gsem.at[1])
    scatter(dstp_ref, half, yb_ref, ssem.at[1])
    ffn(xa_ref, ya_ref)

    wait_gather(xb_ref, gsem.at[1])
    wait_scatter(yb_ref, ssem.at[1])
    gather(tokn_ref, 0, xa_ref, gsem.at[0])
    scatter(dst_ref, 0, ya_ref, ssem.at[0])
    ffn(xb_ref, yb_ref)

    @pl.when(i == n_blocks - 1)
    def _():
        scatter(dst_ref, half, yb_ref, ssem.at[1])
        wait_scatter(ya_ref, ssem.at[0])
        wait_scatter(yb_ref, ssem.at[1])
        wait_gather(xa_ref, gsem.at[0])


def _ffn_call(block_e, n_used, row_tok, dst_ext, h, w1, b1, w2, b2):
    tp, d = h.shape
    n_exp, _, f2 = w1.shape
    f = w2.shape[1]
    nb = row_tok.shape[0]
    rb = EXPERT_ROWS
    half = rb // 2
    n_out = nb * rb + half

    def blk(i, be, nu):
        return jnp.minimum(i, nu[0] - 1)

    def idx_spec(index_map):
        return pl.BlockSpec((1, 1, rb), index_map, memory_space=pltpu.SMEM)

    grid_spec = pltpu.PrefetchScalarGridSpec(
        num_scalar_prefetch=2,
        grid=(nb,),
        in_specs=[idx_spec(lambda i, be, nu: (i, 0, 0)),
                  idx_spec(lambda i, be, nu: (jnp.minimum(i + 1, nb - 1), 0, 0)),
                  idx_spec(lambda i, be, nu: (i, 0, 0)),
                  idx_spec(lambda i, be, nu: (i + 1, 0, 0)),
                  pl.BlockSpec(memory_space=pl.ANY),
                  pl.BlockSpec((1, d, f2), lambda i, be, nu: (be[blk(i, be, nu)], 0, 0)),
                  pl.BlockSpec((1, 1, f2), lambda i, be, nu: (be[blk(i, be, nu)], 0, 0)),
                  pl.BlockSpec((1, f, d), lambda i, be, nu: (be[blk(i, be, nu)], 0, 0)),
                  pl.BlockSpec((1, 1, d), lambda i, be, nu: (be[blk(i, be, nu)], 0, 0))],
        out_specs=pl.BlockSpec(memory_space=pl.ANY),
        scratch_shapes=[pltpu.VMEM((half, d), F32)] * 4 + [
            pltpu.VMEM((d, f2), BF16), pltpu.VMEM((f, d), BF16),
            pltpu.SemaphoreType.DMA((2,)), pltpu.SemaphoreType.DMA((2,))],
    )
    return pl.pallas_call(
        functools.partial(_ffn_kernel, n_blocks=nb),
        grid_spec=grid_spec,
        out_shape=jax.ShapeDtypeStruct((n_out, d), F32),
        compiler_params=_params(("arbitrary",)),
    )(block_e, n_used, row_tok, row_tok, dst_ext, dst_ext, h,
      w1, b1.reshape(n_exp, 1, f2), w2, b2.reshape(n_exp, 1, d))


def _combine_kernel(gate_ref, h_ref, g_ref, b_ref, *refs, alpha):
    y_refs, out_ref = refs[:TOP_K], refs[TOP_K]
    gates = gate_ref[...]
    ffn = gates[:, 0:1] * y_refs[0][...]
    for k in range(1, TOP_K):
        ffn = ffn + gates[:, k:k + 1] * y_refs[k][...]
    out_ref[...] = _layer_norm(alpha * h_ref[...] + ffn, g_ref[...], b_ref[...])


def _combine_call(gates, h, y, ln_g, ln_b, *, alpha):
    tp, d = h.shape
    tile = ROUTE_TILE
    nt = tp // tile
    y_specs = [pl.BlockSpec((tile, d), functools.partial(lambda k, i: (k * nt + i, 0), k)) for k in range(TOP_K)]
    return pl.pallas_call(
        functools.partial(_combine_kernel, alpha=alpha),
        grid=(nt,),
        in_specs=[pl.BlockSpec((tile, TOP_K), lambda i: (i, 0)),
                  pl.BlockSpec((tile, d), lambda i: (i, 0)),
                  pl.BlockSpec((1, d), lambda i: (0, 0)),
                  pl.BlockSpec((1, d), lambda i: (0, 0))] + y_specs,
        out_specs=pl.BlockSpec((tile, d), lambda i: (i, 0)),
        out_shape=jax.ShapeDtypeStruct((tp, d), F32),
        compiler_params=_params(("arbitrary",)),
    )(gates, h, ln_g.reshape(1, d), ln_b.reshape(1, d), *([y] * TOP_K))


def _moe(h, top_i, gates, rank, counts, w1, b1, w2, b2, ln_g, ln_b, *, alpha):
    tp = h.shape[0]
    n_exp = w1.shape[0]
    n_blocks = tp * TOP_K // EXPERT_ROWS + n_exp
    block_e, n_used, row_tok, dst_ext = _plan(counts, top_i, rank, n_blocks)
    y = _ffn_call(block_e, n_used, row_tok, dst_ext, h, w1, b1, w2, b2)
    return _combine_call(gates, h, y, ln_g, ln_b, alpha=alpha)


def _log_sigmoid(x):
    return jnp.minimum(x, 0.0) - jnp.log(1.0 + jnp.exp(-jnp.abs(x)))


def _split3(c):
    hi = c.astype(BF16)
    r1 = c - hi.astype(F32)
    mid = r1.astype(BF16)
    lo = (r1 - mid.astype(F32)).astype(BF16)
    return hi, mid, lo


def _proj_kernel(h_ref, wqt_ref, wk_ref, wvt_ref, wf_ref, wft_ref, bf_ref, bft_ref,
                 selq_ref, selk_ref, oneq_ref, onek_ref, onev_ref,
                 qt_ref, kx_ref, vt_ref, carry_ref, carryt_ref, *, tile, n_heads):
    i = pl.program_id(1)

    @pl.when(i == 0)
    def _():
        carry_ref[...] = jnp.zeros_like(carry_ref)
        carryt_ref[...] = jnp.zeros_like(carryt_ref)

    x = h_ref[...]
    xb = x.astype(BF16)
    nt_dims = (((1,), (1,)), ((), ()))
    lf = _log_sigmoid(jnp.dot(x, wf_ref[...], precision=HIGHEST, preferred_element_type=F32) + bf_ref[...])
    lft = _log_sigmoid(lax.dot_general(wft_ref[...], x, nt_dims, precision=HIGHEST,
                                       preferred_element_type=F32) + bft_ref[...])
    r = lax.broadcasted_iota(jnp.int32, (tile, tile), 0)
    c = lax.broadcasted_iota(jnp.int32, (tile, tile), 1)
    cs = jnp.dot((c <= r).astype(F32), lf, precision=HIGHEST, preferred_element_type=F32) + carry_ref[...]
    cst = jnp.dot(lft, (r <= c).astype(F32), precision=HIGHEST, preferred_element_type=F32) + carryt_ref[...]
    carry_ref[...] = cs[tile - 1:tile, :]
    carryt_ref[...] = cst[:, tile - 1:tile]

    qt = lax.dot_general(wqt_ref[...], xb, nt_dims, preferred_element_type=F32) + oneq_ref[...]
    for p, part in enumerate(_split3(cst * LOG2E)):
        qt = qt + jnp.dot(selq_ref[p], part, preferred_element_type=F32)
    kx = jnp.dot(xb, wk_ref[...], preferred_element_type=F32) + onek_ref[...]
    for p, part in enumerate(_split3(-cs * LOG2E)):
        kx = kx + jnp.dot(part, selk_ref[p], preferred_element_type=F32)
    vt = lax.dot_general(wvt_ref[...], xb, nt_dims, preferred_element_type=F32) + onev_ref[...]
    for hd in range(n_heads):
        qt_ref[0, hd, 0] = qt[hd * LANES:(hd + 1) * LANES, :].astype(BF16)
        kx_ref[0, hd, 0] = kx[:, hd * LANES:(hd + 1) * LANES].astype(BF16)
        vt_ref[0, hd, 0] = vt[hd * LANES:(hd + 1) * LANES, :].astype(BF16)


def _attn_weights(w_in, b_f, n_heads, head_dim):
    d = w_in.shape[0]
    scale = head_dim ** -0.5 * LOG2E
    hw = n_heads * LANES

    def pad_heads(w):
        w = w.reshape(d, n_heads, head_dim)
        return jnp.pad(w, ((0, 0), (0, 0), (0, LANES - head_dim))).reshape(d, hw)

    wqt = pad_heads(w_in[:, :d] * scale).T.astype(BF16)
    wk = pad_heads(w_in[:, d:2 * d]).astype(BF16)
    wvt = pad_heads(w_in[:, 2 * d:3 * d]).T.astype(BF16)
    wf = w_in[:, 3 * d:]
    selq = np.zeros((3, hw, n_heads), np.float32)
    selk = np.zeros((3, n_heads, hw), np.float32)
    oneq = np.zeros((hw, 1), np.float32)
    onek = np.zeros((1, hw), np.float32)
    onev = np.zeros((hw, 1), np.float32)
    for h in range(n_heads):
        base = h * LANES + head_dim
        for p in range(3):
            selq[p, base + p, h] = 1.0
            selk[p, h, base + 3 + p] = 1.0
            oneq[base + 3 + p, 0] = 1.0
            onek[0, base + p] = 1.0
        onev[base, 0] = 1.0
    return (wqt, wk, wvt, wf, wf.T, b_f.reshape(1, n_heads), b_f.reshape(n_heads, 1),
            jnp.asarray(selq, BF16), jnp.asarray(selk, BF16),
            jnp.asarray(oneq), jnp.asarray(onek), jnp.asarray(onev))


def _proj_call(h, weights, *, bsz, lp, n_heads):
    tp, d = h.shape
    tile = SEQ_TILE
    nt = lp // tile
    hw = n_heads * LANES
    row = lambda b, i: (b * nt + i, 0)
    in_specs = [pl.BlockSpec((tile, d), row)] + [_const_spec(w.shape) for w in weights]
    t_spec = pl.BlockSpec((1, n_heads, 1, LANES, tile), lambda b, i: (b, 0, i, 0, 0))
    k_spec = pl.BlockSpec((1, n_heads, 1, tile, LANES), lambda b, i: (b, 0, i, 0, 0))
    return pl.pallas_call(
        functools.partial(_proj_kernel, tile=tile, n_heads=n_heads),
        grid=(bsz, nt),
        in_specs=in_specs,
        out_specs=[t_spec, k_spec, t_spec],
        out_shape=[jax.ShapeDtypeStruct((bsz, n_heads, nt, LANES, tile), BF16),
                   jax.ShapeDtypeStruct((bsz, n_heads, nt, tile, LANES), BF16),
                   jax.ShapeDtypeStruct((bsz, n_heads, nt, LANES, tile), BF16)],
        scratch_shapes=[pltpu.VMEM((1, n_heads), F32), pltpu.VMEM((n_heads, 1), F32)],
        compiler_params=_params(("arbitrary", "arbitrary")),
    )(h, *weights)


def _attn_kernel(qt_ref, kx_ref, vt_ref, o_ref, s0_ref, s1_ref, s2_ref, p0_ref, p1_ref, p2_ref,
                 a0_ref, a1_ref, a2_ref, c0_ref, c1_ref, c2_ref, m_ref, acc_ref, *, nt, tile, head_dim):
    s_refs, p_refs, a_refs = (s0_ref, s1_ref, s2_ref), (p0_ref, p1_ref, p2_ref), (a0_ref, a1_ref, a2_ref)
    c_refs = (c0_ref, c1_ref, c2_ref)
    key = lax.broadcasted_iota(jnp.int32, (tile, tile), 0)
    qry = lax.broadcasted_iota(jnp.int32, (tile, tile), 1)

    def logits(slot, qi, kj):
        s = jnp.dot(kx_ref[0, 0, kj], qt_ref[0, 0, qi], preferred_element_type=F32)
        s_refs[slot][...] = s
        c_refs[slot][...] = jnp.max(s, axis=0, keepdims=True)

    def value_update(slot, kj, out_qi=None):
        acc = a_refs[slot][...] * acc_ref[...] + jnp.dot(
            vt_ref[0, 0, kj], p_refs[slot][...], preferred_element_type=F32)
        acc_ref[...] = acc
        if out_qi is not None:
            o_ref[0, 0, out_qi] = (acc[:head_dim, :] * (1.0 / acc[head_dim:head_dim + 1, :])).astype(BF16)

    def substep(slot, qi, kj, *, diag, write_out):
        if diag:
            nqi, nkj = qi + 1, jnp.int32(1)
        else:
            stay = qi - kj >= 2
            nqi, nkj = jnp.where(stay, qi, qi + 1), jnp.where(stay, kj + 2, 0)
        logits((slot + 2) % ATTN_DEPTH, jnp.minimum(nqi, nt - 1), nkj)
        pkj = jnp.where(kj >= 2, kj - 2, jnp.maximum(qi - 1 - jnp.where(kj == 0, 1, 0), 0))
        out_qi = jnp.where(kj == 1, qi - 1, qi) if write_out else None
        value_update((slot + 1) % ATTN_DEPTH, pkj, out_qi)
        m_old = m_ref[...]
        if diag:
            s = jnp.where(key <= qry, s_refs[slot][...], MASK_VALUE)
            m_new = jnp.maximum(m_old, jnp.max(s, axis=0, keepdims=True))
            p_refs[slot][...] = jnp.exp2(s - m_new).astype(BF16)
            m_ref[...] = jnp.full_like(m_old, MASK_VALUE)
        else:
            m_new = jnp.maximum(m_old, c_refs[slot][...])
            p_refs[slot][...] = jnp.exp2(s_refs[slot][...] - m_new).astype(BF16)
            m_ref[...] = m_new
        a_refs[slot][...] = jnp.exp2(m_old - m_new)

    def q_tile(slot, qi, n_loops, rem):
        def trip(i, carry):
            for j in range(ATTN_DEPTH):
                substep((slot + j) % ATTN_DEPTH, qi, ATTN_DEPTH * i + j, diag=False, write_out=j == 1)
            return carry

        lax.fori_loop(0, n_loops, trip, 0)
        for j in range(rem):
            substep((slot + j) % ATTN_DEPTH, qi, ATTN_DEPTH * n_loops + j, diag=False, write_out=j == 1)
        substep((slot + rem) % ATTN_DEPTH, qi, qi, diag=True, write_out=rem == 1)
        return (slot + rem + 1) % ATTN_DEPTH

    m_ref[...] = jnp.full_like(m_ref, MASK_VALUE)
    acc_ref[...] = jnp.ones_like(acc_ref)
    for slot in range(1, ATTN_DEPTH):
        p_refs[slot][...] = jnp.zeros_like(p_refs[slot])
        a_refs[slot][...] = jnp.ones_like(a_refs[slot])
    logits(0, 0, 0)
    logits(1, 1, 0)

    def group(g, carry):
        slot = 0
        for r in range(ATTN_DEPTH):
            slot = q_tile(slot, ATTN_DEPTH * g + r, g, r)
        assert slot == 0
        return carry

    lax.fori_loop(0, nt // ATTN_DEPTH, group, 0)
    slot = 0
    for qi in range(nt // ATTN_DEPTH * ATTN_DEPTH, nt):
        slot = q_tile(slot, jnp.int32(qi), jnp.int32(qi // ATTN_DEPTH), qi % ATTN_DEPTH)
    value_update((slot + 1) % ATTN_DEPTH, nt - 2)
    value_update((slot + 2) % ATTN_DEPTH, nt - 1, nt - 1)


def _attn_call(qt, kx, vt, *, head_dim):
    bsz, n_heads, nt, _, tile = qt.shape
    return pl.pallas_call(
        functools.partial(_attn_kernel, nt=nt, tile=tile, head_dim=head_dim),
        grid=(bsz, n_heads),
        in_specs=[pl.BlockSpec((1, 1, nt, LANES, tile), lambda b, h: (b, h, 0, 0, 0)),
                  pl.BlockSpec((1, 1, nt, tile, LANES), lambda b, h: (b, h, 0, 0, 0)),
                  pl.BlockSpec((1, 1, nt, LANES, tile), lambda b, h: (b, h, 0, 0, 0))],
        out_specs=pl.BlockSpec((1, 1, nt, head_dim, tile), lambda b, h: (b, h, 0, 0, 0)),
        out_shape=jax.ShapeDtypeStruct((bsz, n_heads, nt, head_dim, tile), BF16),
        scratch_shapes=([pltpu.VMEM((tile, tile), F32)] * ATTN_DEPTH + [pltpu.VMEM((tile, tile), BF16)] * ATTN_DEPTH
                        + [pltpu.VMEM((1, tile), F32)] * (2 * ATTN_DEPTH)
                        + [pltpu.VMEM((1, tile), F32), pltpu.VMEM((LANES, tile), F32)]),
        compiler_params=_params(("arbitrary", "arbitrary")),
    )(qt, kx, vt)


def _oproj_kernel(o_ref, wo_ref, h_ref, g_ref, b_ref, rw_ref, rb_ref,
                  hn_ref, ti_ref, gate_ref, rank_ref, cnt_ref, run_ref, *, alpha):
    bi = pl.program_id(0)
    i = pl.program_id(1)

    @pl.when((bi == 0) & (i == 0))
    def _():
        run_ref[...] = jnp.zeros_like(run_ref)

    n_heads, _, head_dim, tile = o_ref.shape[1:]
    o_t = o_ref[0].reshape(n_heads * head_dim, tile)
    att = lax.dot_general(o_t, wo_ref[...], (((0,), (0,)), ((), ())), preferred_element_type=F32)
    hn = _layer_norm(alpha * h_ref[...] + att, g_ref[...], b_ref[...])
    hn_ref[...] = hn
    _route(hn, rw_ref, rb_ref, run_ref, ti_ref, gate_ref, rank_ref, cnt_ref)


def _oproj_call(o, w_out, h, ln_g, ln_b, router_w, router_b, *, bsz, lp, alpha):
    tp, d = h.shape
    tile = SEQ_TILE
    nt = lp // tile
    n_exp = router_w.shape[1]
    row = lambda b, i: (b * nt + i, 0)
    return pl.pallas_call(
        functools.partial(_oproj_kernel, alpha=alpha),
        grid=(bsz, nt),
        in_specs=[pl.BlockSpec((1,) + o.shape[1:2] + (1,) + o.shape[3:], lambda b, i: (b, 0, i, 0, 0)),
                  _const_spec((d, d)),
                  pl.BlockSpec((tile, d), row),
                  _const_spec((1, d)), _const_spec((1, d)),
                  _const_spec((d, n_exp)), _const_spec((1, n_exp))],
        out_specs=[pl.BlockSpec((tile, d), row)] + _route_out_specs(nt, tile, n_exp),
        out_shape=[jax.ShapeDtypeStruct((tp, d), F32)] + _route_out_shapes(tp, n_exp),
        scratch_shapes=[pltpu.VMEM((1, n_exp), F32)],
        compiler_params=_params(("arbitrary", "arbitrary")),
    )(o, w_out.astype(BF16), h, ln_g.reshape(1, d), ln_b.reshape(1, d),
      router_w, router_b.reshape(1, n_exp))


def kernel(x, meta_tokens, pool_w, pool_scale, attn_w_in, attn_b_f, attn_w_out,
           ln_g, ln_b, router_w, router_b, w1, b1, w2, b2):
    bsz, seq, d = x.shape
    n_meta = meta_tokens.shape[0]
    depth = ln_g.shape[0]
    n_heads = attn_b_f.shape[-1]
    head_dim = d // n_heads
    alpha = float((2 * depth) ** 0.25)
    length = n_meta + seq
    lp = -(-length // SEQ_TILE) * SEQ_TILE
    assert d % (len(POOL_WINDOWS) * LANES) == 0 and head_dim + 6 <= LANES and depth == 2

    meta = jnp.broadcast_to(meta_tokens[None], (bsz, n_meta, d))
    h = jnp.concatenate([meta, x, jnp.zeros((bsz, lp - length, d), x.dtype)], axis=1)
    h = h.reshape(bsz * lp, d)

    h, top_i, gates, rank, counts = _pool_call(
        h, pool_w[0], pool_scale[0], ln_g[0, 0], ln_b[0, 0], router_w[0], router_b[0],
        bsz=bsz, lp=lp, alpha=alpha)
    h = _moe(h, top_i, gates, rank, counts, w1[0], b1[0], w2[0], b2[0], ln_g[0, 1], ln_b[0, 1], alpha=alpha)

    weights = _attn_weights(attn_w_in[0], attn_b_f[0], n_heads, head_dim)
    qt, kx, vt = _proj_call(h, weights, bsz=bsz, lp=lp, n_heads=n_heads)
    o = _attn_call(qt, kx, vt, head_dim=head_dim)
    h, top_i, gates, rank, counts = _oproj_call(
        o, attn_w_out[0], h, ln_g[1, 0], ln_b[1, 0], router_w[1], router_b[1],
        bsz=bsz, lp=lp, alpha=alpha)
    h = _moe(h, top_i, gates, rank, counts, w1[1], b1[1], w2[1], b2[1], ln_g[1, 1], ln_b[1, 1], alpha=alpha)

    return h.reshape(bsz, lp, d)[:, n_meta:length]
```

```python
import functools

import numpy as np
import jax
import jax.numpy as jnp
from jax import lax
from jax.experimental import pallas as pl
from jax.experimental.pallas import tpu as pltpu

POOL_WINDOWS = (2, 4, 8, 16)
MAX_WIN = max(POOL_WINDOWS)
TOP_K = 4
SWIGLU_LIMIT = 7.0
SWIGLU_ALPHA = 1.702
LN_EPS = 1e-5
MASK_VALUE = -1e30

LANES = 128
SEQ_TILE = 512
ROUTE_TILE = 256
EXPERT_ROWS = 512
VMEM_LIMIT = 56 * 1024 * 1024
LOG2E = 1.4426950408889634
ATTN_DEPTH = 3

F32 = jnp.float32
BF16 = jnp.bfloat16
HIGHEST = lax.Precision.HIGHEST


def _params(sem, vmem=VMEM_LIMIT):
    return pltpu.CompilerParams(dimension_semantics=sem, vmem_limit_bytes=vmem)


def _layer_norm(z, g, b):
    mu = jnp.mean(z, axis=-1, keepdims=True)
    d = z - mu
    var = jnp.mean(d * d, axis=-1, keepdims=True)
    return d * lax.rsqrt(var + LN_EPS) * g + b


def _route(hn, rw_ref, rb_ref, run_ref, ti_ref, gate_ref, rank_ref, cnt_ref):
    rows = hn.shape[0]
    logits = jnp.dot(hn, rw_ref[...], precision=HIGHEST, preferred_element_type=F32) + rb_ref[...]
    n_exp = logits.shape[1]
    lane = lax.broadcasted_iota(jnp.int32, (rows, n_exp), 1)
    cur = logits
    vals, idxs, hots = [], [], []
    for _ in range(TOP_K):
        m = jnp.max(cur, axis=-1, keepdims=True)
        idx = jnp.min(jnp.where(cur == m, lane, n_exp), axis=-1, keepdims=True)
        hot = lane == idx
        vals.append(m)
        idxs.append(idx)
        hots.append(hot)
        cur = jnp.where(hot, -jnp.inf, cur)
    exps = [jnp.exp(v - vals[0]) for v in vals]
    denom = exps[0]
    for e in exps[1:]:
        denom = denom + e
    gates = [e / denom for e in exps]
    sel = hots[0].astype(F32)
    for hot in hots[1:]:
        sel = sel + hot.astype(F32)
    r = lax.broadcasted_iota(jnp.int32, (rows, rows), 0)
    c = lax.broadcasted_iota(jnp.int32, (rows, rows), 1)
    tri = (c < r).astype(BF16)
    base = run_ref[...] + jnp.dot(tri, sel.astype(BF16), preferred_element_type=F32)
    ranks = [jnp.sum(jnp.where(hot, base, 0.0), axis=-1, keepdims=True) for hot in hots]
    run_ref[...] = run_ref[...] + jnp.sum(sel, axis=0, keepdims=True)
    cnt_ref[...] = run_ref[...]
    lane_k = lax.broadcasted_iota(jnp.int32, (rows, TOP_K), 1)

    def pack(cols):
        out = jnp.broadcast_to(cols[TOP_K - 1], (rows, TOP_K))
        for k in range(TOP_K - 2, -1, -1):
            out = jnp.where(lane_k == k, cols[k], out)
        return out

    ti_ref[...] = pack(idxs)
    gate_ref[...] = pack(gates)
    rank_ref[...] = pack(ranks).astype(jnp.int32)


def _pool_kernel(h_ref, pw_ref, ps_ref, g_ref, b_ref, rw_ref, rb_ref,
                 h1_ref, h1t_ref, ti_ref, gate_ref, rank_ref, cnt_ref,
                 ext_ref, run_ref, *, tile, alpha):
    bi = pl.program_id(0)
    i = pl.program_id(1)
    d_model = h_ref.shape[1]
    gdim = d_model // len(POOL_WINDOWS)

    @pl.when(i == 0)
    def _():
        ext_ref[0:MAX_WIN, :] = jnp.zeros((MAX_WIN, d_model), F32)

    @pl.when((bi == 0) & (i == 0))
    def _():
        run_ref[...] = jnp.zeros_like(run_ref)

    x = h_ref[...]
    ext_ref[MAX_WIN:MAX_WIN + tile, :] = x
    pos = i * tile + lax.broadcasted_iota(jnp.int32, (tile, 1), 0)
    ys = []
    for g, w in enumerate(POOL_WINDOWS):
        lo, hi = g * gdim, (g + 1) * gdim
        xg = x[:, lo:hi]
        s = xg
        for j in range(1, w):
            s = s + ext_ref[MAX_WIN - j:MAX_WIN - j + tile, lo:hi]
        cnt = jnp.minimum(pos + 1, w).astype(F32)
        u = s / cnt - xg
        ys.append(jnp.dot(u.astype(BF16), pw_ref[g].astype(BF16), preferred_element_type=F32))
    y = jnp.concatenate(ys, axis=-1) * ps_ref[...]
    hn = _layer_norm(alpha * x + y, g_ref[...], b_ref[...])
    h1_ref[...] = hn
    _store_row_tiles(h1t_ref, hn)
    ext_ref[0:MAX_WIN, :] = x[tile - MAX_WIN:tile, :]
    _route(hn, rw_ref, rb_ref, run_ref, ti_ref, gate_ref, rank_ref, cnt_ref)


def _store_row_tiles(out_ref, x):
    for c in range(out_ref.shape[1]):
        out_ref[:, c, :] = x[:, c * LANES:(c + 1) * LANES]


def _load_row_tiles(ref):
    return jnp.concatenate([ref[:, c, :] for c in range(ref.shape[1])], axis=-1)


def _route_out_shapes(tp, n_exp):
    return [jax.ShapeDtypeStruct((tp, TOP_K), jnp.int32),
            jax.ShapeDtypeStruct((tp, TOP_K), F32),
            jax.ShapeDtypeStruct((tp, TOP_K), jnp.int32),
            jax.ShapeDtypeStruct((1, n_exp), F32)]


def _route_out_specs(nt, tile, n_exp):
    row = lambda b, i: (b * nt + i, 0)
    return [pl.BlockSpec((tile, TOP_K), row),
            pl.BlockSpec((tile, TOP_K), row),
            pl.BlockSpec((tile, TOP_K), row),
            pl.BlockSpec((1, n_exp), lambda b, i: (0, 0))]


def _row_tile_spec(nt, tile, d):
    return pl.BlockSpec((tile, d // LANES, LANES), lambda b, i: (b * nt + i, 0, 0))


def _const_spec(shape):
    return pl.BlockSpec(shape, lambda b, i: (0,) * len(shape))


def _pool_call(h, pool_w, pool_scale, ln_g, ln_b, router_w, router_b, *, bsz, lp, alpha):
    tp, d = h.shape
    tile = SEQ_TILE
    nt = lp // tile
    n_exp = router_w.shape[1]
    groups, gdim, _ = pool_w.shape
    row = lambda b, i: (b * nt + i, 0)
    return pl.pallas_call(
        functools.partial(_pool_kernel, tile=tile, alpha=alpha),
        grid=(bsz, nt),
        in_specs=[pl.BlockSpec((tile, d), row),
                  _const_spec((groups, gdim, gdim)),
                  _const_spec((1, d)), _const_spec((1, d)), _const_spec((1, d)),
                  _const_spec((d, n_exp)), _const_spec((1, n_exp))],
        out_specs=[pl.BlockSpec((tile, d), row), _row_tile_spec(nt, tile, d)] + _route_out_specs(nt, tile, n_exp),
        out_shape=[jax.ShapeDtypeStruct((tp, d), F32), jax.ShapeDtypeStruct((tp, d // LANES, LANES), F32)]
        + _route_out_shapes(tp, n_exp),
        scratch_shapes=[pltpu.VMEM((MAX_WIN + tile, d), F32), pltpu.VMEM((1, n_exp), F32)],
        compiler_params=_params(("arbitrary", "arbitrary")),
    )(h, pool_w, pool_scale.reshape(1, d), ln_g.reshape(1, d), ln_b.reshape(1, d),
      router_w, router_b.reshape(1, n_exp))


def _plan(counts, top_i, rank, n_blocks):
    tp = top_i.shape[0]
    n_exp = counts.shape[1]
    n_rows = n_blocks * EXPERT_ROWS
    cnt = counts.reshape(n_exp).astype(jnp.int32)
    padded = (cnt + EXPERT_ROWS - 1) // EXPERT_ROWS * EXPERT_ROWS
    pad_ends = jnp.cumsum(padded)
    pad_starts = pad_ends - padded
    hot = top_i[..., None] == jnp.arange(n_exp, dtype=jnp.int32)
    dest = (jnp.sum(jnp.where(hot, pad_starts, 0), axis=-1) + rank).reshape(-1)
    starts = jnp.arange(n_blocks, dtype=jnp.int32) * EXPERT_ROWS
    block_e = jnp.sum((starts[:, None] >= pad_ends[None, :]).astype(jnp.int32), axis=1)
    block_e = jnp.minimum(block_e, n_exp - 1)
    n_used = (pad_ends[-1] // EXPERT_ROWS).reshape(1)

    copy = jnp.arange(tp * TOP_K, dtype=jnp.int32)
    row_copy = jnp.full((n_rows,), -1, jnp.int32).at[dest].set(copy, unique_indices=True)
    valid = row_copy >= 0
    tok, k = row_copy // TOP_K, row_copy % TOP_K
    scrap = tp * TOP_K + jnp.cumsum(jnp.logical_not(valid).astype(jnp.int32)) - 1
    row_tok = jnp.where(valid, tok, 0)
    row_dst = jnp.where(valid, k * tp + tok, scrap)
    first = n_rows + jnp.arange(EXPERT_ROWS, dtype=jnp.int32) % (EXPERT_ROWS // 2)
    dst_ext = jnp.concatenate([first, row_dst])
    return (block_e, n_used, row_tok.reshape(n_blocks, 1, EXPERT_ROWS),
            dst_ext.reshape(n_blocks + 1, 1, EXPERT_ROWS))


def _ffn_kernel(be_ref, nu_ref, tok_ref, tokn_ref, dstp_ref, dst_ref, h_ref, w1_ref, b1_ref, w2_ref, b2_ref,
                y_ref, xa_ref, xb_ref, ya_ref, yb_ref, w1b_ref, w2b_ref, gsem, ssem):
    i = pl.program_id(0)
    half = xa_ref.shape[0]
    f = w2_ref.shape[1]

    def gather(idx_ref, base, buf_ref, sem):
        for r in range(half):
            pltpu.make_async_copy(h_ref.at[idx_ref[0, 0, base + r]], buf_ref.at[r], sem).start()

    def scatter(idx_ref, base, buf_ref, sem):
        for r in range(half):
            pltpu.make_async_copy(buf_ref.at[r], y_ref.at[idx_ref[0, 0, base + r]], sem).start()

    def wait_gather(buf_ref, sem):
        pltpu.make_async_copy(h_ref.at[pl.ds(0, half)], buf_ref, sem).wait()

    def wait_scatter(buf_ref, sem):
        pltpu.make_async_copy(buf_ref, y_ref.at[pl.ds(0, half)], sem).wait()

    def ffn(x_ref, out_ref):
        hid = jnp.dot(_load_row_tiles(x_ref).astype(BF16), w1b_ref[...], preferred_element_type=F32) + b1_ref[0]
        gate = jnp.minimum(hid[:, :f], SWIGLU_LIMIT)
        up = jnp.clip(hid[:, f:], -SWIGLU_LIMIT, SWIGLU_LIMIT)
        act = gate * jax.nn.sigmoid(SWIGLU_ALPHA * gate) * (up + 1.0)
        _store_row_tiles(out_ref, jnp.dot(act.astype(BF16), w2b_ref[...], preferred_element_type=F32) + b2_ref[0])

    @pl.when(i < nu_ref[0])
    def _():
        @pl.when(i == 0)
        def _():
            yb_ref[...] = jnp.zeros_like(yb_ref)
            gather(tok_ref, 0, xa_ref, gsem.at[0])

        @pl.when((i == 0) | (be_ref[i] != be_ref[jnp.maximum(i - 1, 0)]))
        def _():
            w1b_ref[...] = w1_ref[0].astype(BF16)
            w2b_ref[...] = w2_ref[0].astype(BF16)

        @pl.when(i >= 1)
        def _():
            wait_scatter(ya_ref, ssem.at[0])

        wait_gather(xa_ref, gsem.at[0])
        gather(tok_ref, half, xb_ref, gsem.at[1])
        scatter(dstp_ref, half, yb_ref, ssem.at[1])
        ffn(xa_ref, ya_ref)

        wait_gather(xb_ref, gsem.at[1])
        wait_scatter(yb_ref, ssem.at[1])
        gather(tokn_ref, 0, xa_ref, gsem.at[0])
        scatter(dst_ref, 0, ya_ref, ssem.at[0])
        ffn(xb_ref, yb_ref)

        @pl.when(i == nu_ref[0] - 1)
        def _():
            scatter(dst_ref, half, yb_ref, ssem.at[1])
            wait_scatter(ya_ref, ssem.at[0])
            wait_scatter(yb_ref, ssem.at[1])
            wait_gather(xa_ref, gsem.at[0])

    @pl.when(i >= nu_ref[0])
    def _():
        ya_ref[...] = jnp.zeros_like(ya_ref)
        first = dst_ref[0, 0, 0]
        copies = [pltpu.make_async_copy(ya_ref, y_ref.at[pl.ds(first + j * half, half)], ssem.at[j])
                  for j in range(2)]
        for cp in copies:
            cp.start()
        for cp in copies:
            cp.wait()


def _ffn_call(block_e, n_used, row_tok, dst_ext, h_tiles, w1, b1, w2, b2):
    tp, chunks, _ = h_tiles.shape
    d = chunks * LANES
    n_exp, _, f2 = w1.shape
    f = w2.shape[1]
    nb = row_tok.shape[0]
    rb = EXPERT_ROWS
    half = rb // 2
    n_out = nb * rb + half

    def blk(i, be, nu):
        return jnp.minimum(i, nu[0] - 1)

    def idx_spec(index_map):
        return pl.BlockSpec((1, 1, rb), index_map, memory_space=pltpu.SMEM)

    grid_spec = pltpu.PrefetchScalarGridSpec(
        num_scalar_prefetch=2,
        grid=(nb,),
        in_specs=[idx_spec(lambda i, be, nu: (i, 0, 0)),
                  idx_spec(lambda i, be, nu: (jnp.minimum(i + 1, nb - 1), 0, 0)),
                  idx_spec(lambda i, be, nu: (i, 0, 0)),
                  idx_spec(lambda i, be, nu: (i + 1, 0, 0)),
                  pl.BlockSpec(memory_space=pl.ANY),
                  pl.BlockSpec((1, d, f2), lambda i, be, nu: (be[blk(i, be, nu)], 0, 0)),
                  pl.BlockSpec((1, 1, f2), lambda i, be, nu: (be[blk(i, be, nu)], 0, 0)),
                  pl.BlockSpec((1, f, d), lambda i, be, nu: (be[blk(i, be, nu)], 0, 0)),
                  pl.BlockSpec((1, 1, d), lambda i, be, nu: (be[blk(i, be, nu)], 0, 0))],
        out_specs=pl.BlockSpec(memory_space=pl.ANY),
        scratch_shapes=[pltpu.VMEM((half, chunks, LANES), F32)] * 4 + [
            pltpu.VMEM((d, f2), BF16), pltpu.VMEM((f, d), BF16),
            pltpu.SemaphoreType.DMA((2,)), pltpu.SemaphoreType.DMA((2,))],
    )
    return pl.pallas_call(
        _ffn_kernel,
        grid_spec=grid_spec,
        out_shape=jax.ShapeDtypeStruct((n_out, chunks, LANES), F32),
        compiler_params=_params(("arbitrary",)),
    )(block_e, n_used, row_tok, row_tok, dst_ext, dst_ext, h_tiles,
      w1, b1.reshape(n_exp, 1, f2), w2, b2.reshape(n_exp, 1, d))


def _combine_kernel(gate_ref, h_ref, g_ref, b_ref, *refs, alpha):
    y_refs, out_ref = refs[:TOP_K], refs[TOP_K]
    gates = gate_ref[...]
    ffn = gates[:, 0:1] * _load_row_tiles(y_refs[0])
    for k in range(1, TOP_K):
        ffn = ffn + gates[:, k:k + 1] * _load_row_tiles(y_refs[k])
    out_ref[...] = _layer_norm(alpha * h_ref[...] + ffn, g_ref[...], b_ref[...])


def _combine_call(gates, h, y, ln_g, ln_b, *, alpha):
    tp, d = h.shape
    tile = ROUTE_TILE
    nt = tp // tile
    y_specs = [pl.BlockSpec((tile,) + y.shape[1:], functools.partial(lambda k, i: (k * nt + i, 0, 0), k))
               for k in range(TOP_K)]
    return pl.pallas_call(
        functools.partial(_combine_kernel, alpha=alpha),
        grid=(nt,),
        in_specs=[pl.BlockSpec((tile, TOP_K), lambda i: (i, 0)),
                  pl.BlockSpec((tile, d), lambda i: (i, 0)),
                  pl.BlockSpec((1, d), lambda i: (0, 0)),
                  pl.BlockSpec((1, d), lambda i: (0, 0))] + y_specs,
        out_specs=pl.BlockSpec((tile, d), lambda i: (i, 0)),
        out_shape=jax.ShapeDtypeStruct((tp, d), F32),
        compiler_params=_params(("arbitrary",)),
    )(gates, h, ln_g.reshape(1, d), ln_b.reshape(1, d), *([y] * TOP_K))


def _moe(h, h_tiles, top_i, gates, rank, counts, w1, b1, w2, b2, ln_g, ln_b, *, alpha):
    tp = h.shape[0]
    n_exp = w1.shape[0]
    n_blocks = tp * TOP_K // EXPERT_ROWS + n_exp
    block_e, n_used, row_tok, dst_ext = _plan(counts, top_i, rank, n_blocks)
    y = _ffn_call(block_e, n_used, row_tok, dst_ext, h_tiles, w1, b1, w2, b2)
    return _combine_call(gates, h, y, ln_g, ln_b, alpha=alpha)


def _log_sigmoid(x):
    return jnp.minimum(x, 0.0) - jnp.log(1.0 + jnp.exp(-jnp.abs(x)))


def _split3(c):
    hi = c.astype(BF16)
    r1 = c - hi.astype(F32)
    mid = r1.astype(BF16)
    lo = (r1 - mid.astype(F32)).astype(BF16)
    return hi, mid, lo


def _proj_kernel(h_ref, wqt_ref, wk_ref, wvt_ref, wf_ref, wft_ref, bf_ref, bft_ref,
                 selq_ref, selk_ref, oneq_ref, onek_ref, onev_ref,
                 qt_ref, kx_ref, vt_ref, carry_ref, carryt_ref, *, tile, n_heads):
    i = pl.program_id(1)

    @pl.when(i == 0)
    def _():
        carry_ref[...] = jnp.zeros_like(carry_ref)
        carryt_ref[...] = jnp.zeros_like(carryt_ref)

    x = h_ref[...]
    xb = x.astype(BF16)
    nt_dims = (((1,), (1,)), ((), ()))
    lf = _log_sigmoid(jnp.dot(x, wf_ref[...], precision=HIGHEST, preferred_element_type=F32) + bf_ref[...])
    lft = _log_sigmoid(lax.dot_general(wft_ref[...], x, nt_dims, precision=HIGHEST,
                                       preferred_element_type=F32) + bft_ref[...])
    r = lax.broadcasted_iota(jnp.int32, (tile, tile), 0)
    c = lax.broadcasted_iota(jnp.int32, (tile, tile), 1)
    cs = jnp.dot((c <= r).astype(F32), lf, precision=HIGHEST, preferred_element_type=F32) + carry_ref[...]
    cst = jnp.dot(lft, (r <= c).astype(F32), precision=HIGHEST, preferred_element_type=F32) + carryt_ref[...]
    carry_ref[...] = cs[tile - 1:tile, :]
    carryt_ref[...] = cst[:, tile - 1:tile]

    qt = lax.dot_general(wqt_ref[...], xb, nt_dims, preferred_element_type=F32) + oneq_ref[...]
    for p, part in enumerate(_split3(cst * LOG2E)):
        qt = qt + jnp.dot(selq_ref[p], part, preferred_element_type=F32)
    kx = jnp.dot(xb, wk_ref[...], preferred_element_type=F32) + onek_ref[...]
    for p, part in enumerate(_split3(-cs * LOG2E)):
        kx = kx + jnp.dot(part, selk_ref[p], preferred_element_type=F32)
    vt = lax.dot_general(wvt_ref[...], xb, nt_dims, preferred_element_type=F32) + onev_ref[...]
    for hd in range(n_heads):
        qt_ref[0, hd, 0] = qt[hd * LANES:(hd + 1) * LANES, :].astype(BF16)
        kx_ref[0, hd, 0] = kx[:, hd * LANES:(hd + 1) * LANES].astype(BF16)
        vt_ref[0, hd, 0] = vt[hd * LANES:(hd + 1) * LANES, :].astype(BF16)


def _attn_weights(w_in, b_f, n_heads, head_dim):
    d = w_in.shape[0]
    scale = head_dim ** -0.5 * LOG2E
    hw = n_heads * LANES

    def pad_heads(w):
        w = w.reshape(d, n_heads, head_dim)
        return jnp.pad(w, ((0, 0), (0, 0), (0, LANES - head_dim))).reshape(d, hw)

    wqt = pad_heads(w_in[:, :d] * scale).T.astype(BF16)
    wk = pad_heads(w_in[:, d:2 * d]).astype(BF16)
    wvt = pad_heads(w_in[:, 2 * d:3 * d]).T.astype(BF16)
    wf = w_in[:, 3 * d:]
    selq = np.zeros((3, hw, n_heads), np.float32)
    selk = np.zeros((3, n_heads, hw), np.float32)
    oneq = np.zeros((hw, 1), np.float32)
    onek = np.zeros((1, hw), np.float32)
    onev = np.zeros((hw, 1), np.float32)
    for h in range(n_heads):
        base = h * LANES + head_dim
        for p in range(3):
            selq[p, base + p, h] = 1.0
            selk[p, h, base + 3 + p] = 1.0
            oneq[base + 3 + p, 0] = 1.0
            onek[0, base + p] = 1.0
        onev[base, 0] = 1.0
    return (wqt, wk, wvt, wf, wf.T, b_f.reshape(1, n_heads), b_f.reshape(n_heads, 1),
            jnp.asarray(selq, BF16), jnp.asarray(selk, BF16),
            jnp.asarray(oneq), jnp.asarray(onek), jnp.asarray(onev))


def _proj_call(h, weights, *, bsz, lp, n_heads):
    tp, d = h.shape
    tile = SEQ_TILE
    nt = lp // tile
    hw = n_heads * LANES
    row = lambda b, i: (b * nt + i, 0)
    in_specs = [pl.BlockSpec((tile, d), row)] + [_const_spec(w.shape) for w in weights]
    t_spec = pl.BlockSpec((1, n_heads, 1, LANES, tile), lambda b, i: (b, 0, i, 0, 0))
    k_spec = pl.BlockSpec((1, n_heads, 1, tile, LANES), lambda b, i: (b, 0, i, 0, 0))
    return pl.pallas_call(
        functools.partial(_proj_kernel, tile=tile, n_heads=n_heads),
        grid=(bsz, nt),
        in_specs=in_specs,
        out_specs=[t_spec, k_spec, t_spec],
        out_shape=[jax.ShapeDtypeStruct((bsz, n_heads, nt, LANES, tile), BF16),
                   jax.ShapeDtypeStruct((bsz, n_heads, nt, tile, LANES), BF16),
                   jax.ShapeDtypeStruct((bsz, n_heads, nt, LANES, tile), BF16)],
        scratch_shapes=[pltpu.VMEM((1, n_heads), F32), pltpu.VMEM((n_heads, 1), F32)],
        compiler_params=_params(("arbitrary", "arbitrary")),
    )(h, *weights)


def _attn_kernel(qt_ref, kx_ref, vt_ref, o_ref, s0_ref, s1_ref, s2_ref, p0_ref, p1_ref, p2_ref,
                 a0_ref, a1_ref, a2_ref, c0_ref, c1_ref, c2_ref, m_ref, acc_ref, *, nt, tile, head_dim):
    s_refs, p_refs, a_refs = (s0_ref, s1_ref, s2_ref), (p0_ref, p1_ref, p2_ref), (a0_ref, a1_ref, a2_ref)
    c_refs = (c0_ref, c1_ref, c2_ref)
    key = lax.broadcasted_iota(jnp.int32, (tile, tile), 0)
    qry = lax.broadcasted_iota(jnp.int32, (tile, tile), 1)

    def logits(slot, qi, kj):
        s = jnp.dot(kx_ref[0, 0, kj], qt_ref[0, 0, qi], preferred_element_type=F32)
        s_refs[slot][...] = s
        c_refs[slot][...] = jnp.max(s, axis=0, keepdims=True)

    def value_update(slot, kj, out_qi=None):
        acc = a_refs[slot][...] * acc_ref[...] + jnp.dot(
            vt_ref[0, 0, kj], p_refs[slot][...], preferred_element_type=F32)
        acc_ref[...] = acc
        if out_qi is not None:
            o_ref[0, 0, out_qi] = (acc[:head_dim, :] * (1.0 / acc[head_dim:head_dim + 1, :])).astype(BF16)

    def substep(slot, qi, kj, *, diag, write_out):
        if diag:
            nqi, nkj = qi + 1, jnp.int32(1)
        else:
            stay = qi - kj >= 2
            nqi, nkj = jnp.where(stay, qi, qi + 1), jnp.where(stay, kj + 2, 0)
        logits((slot + 2) % ATTN_DEPTH, jnp.minimum(nqi, nt - 1), nkj)
        pkj = jnp.where(kj >= 2, kj - 2, jnp.maximum(qi - 1 - jnp.where(kj == 0, 1, 0), 0))
        out_qi = jnp.where(kj == 1, qi - 1, qi) if write_out else None
        value_update((slot + 1) % ATTN_DEPTH, pkj, out_qi)
        m_old = m_ref[...]
        if diag:
            s = jnp.where(key <= qry, s_refs[slot][...], MASK_VALUE)
            m_new = jnp.maximum(m_old, jnp.max(s, axis=0, keepdims=True))
            p_refs[slot][...] = jnp.exp2(s - m_new).astype(BF16)
            m_ref[...] = jnp.full_like(m_old, MASK_VALUE)
        else:
            m_new = jnp.maximum(m_old, c_refs[slot][...])
            p_refs[slot][...] = jnp.exp2(s_refs[slot][...] - m_new).astype(BF16)
            m_ref[...] = m_new
        a_refs[slot][...] = jnp.exp2(m_old - m_new)

    def q_tile(slot, qi, n_loops, rem):
        def trip(i, carry):
            for j in range(ATTN_DEPTH):
                substep((slot + j) % ATTN_DEPTH, qi, ATTN_DEPTH * i + j, diag=False, write_out=j == 1)
            return carry

        lax.fori_loop(0, n_loops, trip, 0)
        for j in range(rem):
            substep((slot + j) % ATTN_DEPTH, qi, ATTN_DEPTH * n_loops + j, diag=False, write_out=j == 1)
        substep((slot + rem) % ATTN_DEPTH, qi, qi, diag=True, write_out=rem == 1)
        return (slot + rem + 1) % ATTN_DEPTH

    m_ref[...] = jnp.full_like(m_ref, MASK_VALUE)
    acc_ref[...] = jnp.ones_like(acc_ref)
    for slot in range(1, ATTN_DEPTH):
        p_refs[slot][...] = jnp.zeros_like(p_refs[slot])
        a_refs[slot][...] = jnp.ones_like(a_refs[slot])
    logits(0, 0, 0)
    logits(1, 1, 0)

    def group(g, carry):
        slot = 0
        for r in range(ATTN_DEPTH):
            slot = q_tile(slot, ATTN_DEPTH * g + r, g, r)
        assert slot == 0
        return carry

    lax.fori_loop(0, nt // ATTN_DEPTH, group, 0)
    slot = 0
    for qi in range(nt // ATTN_DEPTH * ATTN_DEPTH, nt):
        slot = q_tile(slot, jnp.int32(qi), jnp.int32(qi // ATTN_DEPTH), qi % ATTN_DEPTH)
    value_update((slot + 1) % ATTN_DEPTH, nt - 2)
    value_update((slot + 2) % ATTN_DEPTH, nt - 1, nt - 1)


def _attn_call(qt, kx, vt, *, head_dim):
    bsz, n_heads, nt, _, tile = qt.shape
    return pl.pallas_call(
        functools.partial(_attn_kernel, nt=nt, tile=tile, head_dim=head_dim),
        grid=(bsz, n_heads),
        in_specs=[pl.BlockSpec((1, 1, nt, LANES, tile), lambda b, h: (b, h, 0, 0, 0)),
                  pl.BlockSpec((1, 1, nt, tile, LANES), lambda b, h: (b, h, 0, 0, 0)),
                  pl.BlockSpec((1, 1, nt, LANES, tile), lambda b, h: (b, h, 0, 0, 0))],
        out_specs=pl.BlockSpec((1, 1, nt, head_dim, tile), lambda b, h: (b, h, 0, 0, 0)),
        out_shape=jax.ShapeDtypeStruct((bsz, n_heads, nt, head_dim, tile), BF16),
        scratch_shapes=([pltpu.VMEM((tile, tile), F32)] * ATTN_DEPTH + [pltpu.VMEM((tile, tile), BF16)] * ATTN_DEPTH
                        + [pltpu.VMEM((1, tile), F32)] * (2 * ATTN_DEPTH)
                        + [pltpu.VMEM((1, tile), F32), pltpu.VMEM((LANES, tile), F32)]),
        compiler_params=_params(("arbitrary", "arbitrary")),
    )(qt, kx, vt)


def _oproj_kernel(o_ref, wo_ref, h_ref, g_ref, b_ref, rw_ref, rb_ref,
                  hn_ref, hnt_ref, ti_ref, gate_ref, rank_ref, cnt_ref, run_ref, *, alpha):
    bi = pl.program_id(0)
    i = pl.program_id(1)

    @pl.when((bi == 0) & (i == 0))
    def _():
        run_ref[...] = jnp.zeros_like(run_ref)

    n_heads, _, head_dim, tile = o_ref.shape[1:]
    o_t = o_ref[0].reshape(n_heads * head_dim, tile)
    att = lax.dot_general(o_t, wo_ref[...], (((0,), (0,)), ((), ())), preferred_element_type=F32)
    hn = _layer_norm(alpha * h_ref[...] + att, g_ref[...], b_ref[...])
    hn_ref[...] = hn
    _store_row_tiles(hnt_ref, hn)
    _route(hn, rw_ref, rb_ref, run_ref, ti_ref, gate_ref, rank_ref, cnt_ref)


def _oproj_call(o, w_out, h, ln_g, ln_b, router_w, router_b, *, bsz, lp, alpha):
    tp, d = h.shape
    tile = SEQ_TILE
    nt = lp // tile
    n_exp = router_w.shape[1]
    row = lambda b, i: (b * nt + i, 0)
    return pl.pallas_call(
        functools.partial(_oproj_kernel, alpha=alpha),
        grid=(bsz, nt),
        in_specs=[pl.BlockSpec((1,) + o.shape[1:2] + (1,) + o.shape[3:], lambda b, i: (b, 0, i, 0, 0)),
                  _const_spec((d, d)),
                  pl.BlockSpec((tile, d), row),
                  _const_spec((1, d)), _const_spec((1, d)),
                  _const_spec((d, n_exp)), _const_spec((1, n_exp))],
        out_specs=[pl.BlockSpec((tile, d), row), _row_tile_spec(nt, tile, d)] + _route_out_specs(nt, tile, n_exp),
        out_shape=[jax.ShapeDtypeStruct((tp, d), F32), jax.ShapeDtypeStruct((tp, d // LANES, LANES), F32)]
        + _route_out_shapes(tp, n_exp),
        scratch_shapes=[pltpu.VMEM((1, n_exp), F32)],
        compiler_params=_params(("arbitrary", "arbitrary")),
    )(o, w_out.astype(BF16), h, ln_g.reshape(1, d), ln_b.reshape(1, d),
      router_w, router_b.reshape(1, n_exp))


def kernel(x, meta_tokens, pool_w, pool_scale, attn_w_in, attn_b_f, attn_w_out,
           ln_g, ln_b, router_w, router_b, w1, b1, w2, b2):
    bsz, seq, d = x.shape
    n_meta = meta_tokens.shape[0]
    depth = ln_g.shape[0]
    n_heads = attn_b_f.shape[-1]
    head_dim = d // n_heads
    alpha = float((2 * depth) ** 0.25)
    length = n_meta + seq
    lp = -(-length // SEQ_TILE) * SEQ_TILE
    assert d % (len(POOL_WINDOWS) * LANES) == 0 and head_dim + 6 <= LANES and depth == 2

    meta = jnp.broadcast_to(meta_tokens[None], (bsz, n_meta, d))
    h = jnp.concatenate([meta, x, jnp.zeros((bsz, lp - length, d), x.dtype)], axis=1)
    h = h.reshape(bsz * lp, d)

    h, h_tiles, top_i, gates, rank, counts = _pool_call(
        h, pool_w[0], pool_scale[0], ln_g[0, 0], ln_b[0, 0], router_w[0], router_b[0],
        bsz=bsz, lp=lp, alpha=alpha)
    h = _moe(h, h_tiles, top_i, gates, rank, counts, w1[0], b1[0], w2[0], b2[0], ln_g[0, 1], ln_b[0, 1], alpha=alpha)

    weights = _attn_weights(attn_w_in[0], attn_b_f[0], n_heads, head_dim)
    qt, kx, vt = _proj_call(h, weights, bsz=bsz, lp=lp, n_heads=n_heads)
    o = _attn_call(qt, kx, vt, head_dim=head_dim)
    h, h_tiles, top_i, gates, rank, counts = _oproj_call(
        o, attn_w_out[0], h, ln_g[1, 0], ln_b[1, 0], router_w[1], router_b[1],
        bsz=bsz, lp=lp, alpha=alpha)
    h = _moe(h, h_tiles, top_i, gates, rank, counts, w1[1], b1[1], w2[1], b2[1], ln_g[1, 1], ln_b[1, 1], alpha=alpha)

    return h.reshape(bsz, lp, d)[:, n_meta:length]
```

```python
import functools

import numpy as np
import jax
import jax.numpy as jnp
from jax import lax
from jax.experimental import pallas as pl
from jax.experimental.pallas import tpu as pltpu

POOL_WINDOWS = (2, 4, 8, 16)
MAX_WIN = max(POOL_WINDOWS)
TOP_K = 4
SWIGLU_LIMIT = 7.0
SWIGLU_ALPHA = 1.702
LN_EPS = 1e-5
MASK_VALUE = -1e30

LANES = 128
SEQ_TILE = 512
ROUTE_TILE = 256
EXPERT_ROWS = 512
VMEM_LIMIT = 56 * 1024 * 1024
LOG2E = 1.4426950408889634
ATTN_DEPTH = 3

F32 = jnp.float32
BF16 = jnp.bfloat16
HIGHEST = lax.Precision.HIGHEST


def _params(sem, vmem=VMEM_LIMIT):
    return pltpu.CompilerParams(dimension_semantics=sem, vmem_limit_bytes=vmem)


def _layer_norm(z, g, b):
    mu = jnp.mean(z, axis=-1, keepdims=True)
    d = z - mu
    var = jnp.mean(d * d, axis=-1, keepdims=True)
    return d * lax.rsqrt(var + LN_EPS) * g + b


def _route(hn, rw_ref, rb_ref, run_ref, ti_ref, gate_ref, rank_ref, cnt_ref):
    rows = hn.shape[0]
    logits = jnp.dot(hn, rw_ref[...], precision=HIGHEST, preferred_element_type=F32) + rb_ref[...]
    n_exp = logits.shape[1]
    lane = lax.broadcasted_iota(jnp.int32, (rows, n_exp), 1)
    cur = logits
    vals, idxs, hots = [], [], []
    for _ in range(TOP_K):
        m = jnp.max(cur, axis=-1, keepdims=True)
        idx = jnp.min(jnp.where(cur == m, lane, n_exp), axis=-1, keepdims=True)
        hot = lane == idx
        vals.append(m)
        idxs.append(idx)
        hots.append(hot)
        cur = jnp.where(hot, -jnp.inf, cur)
    exps = [jnp.exp(v - vals[0]) for v in vals]
    denom = exps[0]
    for e in exps[1:]:
        denom = denom + e
    gates = [e / denom for e in exps]
    sel = hots[0].astype(F32)
    for hot in hots[1:]:
        sel = sel + hot.astype(F32)
    r = lax.broadcasted_iota(jnp.int32, (rows, rows), 0)
    c = lax.broadcasted_iota(jnp.int32, (rows, rows), 1)
    tri = (c < r).astype(BF16)
    base = run_ref[...] + jnp.dot(tri, sel.astype(BF16), preferred_element_type=F32)
    ranks = [jnp.sum(jnp.where(hot, base, 0.0), axis=-1, keepdims=True) for hot in hots]
    run_ref[...] = run_ref[...] + jnp.sum(sel, axis=0, keepdims=True)
    cnt_ref[...] = run_ref[...]
    lane_k = lax.broadcasted_iota(jnp.int32, (rows, TOP_K), 1)

    def pack(cols):
        out = jnp.broadcast_to(cols[TOP_K - 1], (rows, TOP_K))
        for k in range(TOP_K - 2, -1, -1):
            out = jnp.where(lane_k == k, cols[k], out)
        return out

    ti_ref[...] = pack(idxs)
    gate_ref[...] = pack(gates)
    rank_ref[...] = pack(ranks).astype(jnp.int32)


def _pool_kernel(h_ref, pw_ref, ps_ref, g_ref, b_ref, rw_ref, rb_ref,
                 h1_ref, h1t_ref, ti_ref, gate_ref, rank_ref, cnt_ref,
                 ext_ref, run_ref, *, tile, alpha):
    bi = pl.program_id(0)
    i = pl.program_id(1)
    d_model = h_ref.shape[1]
    gdim = d_model // len(POOL_WINDOWS)

    @pl.when(i == 0)
    def _():
        ext_ref[0:MAX_WIN, :] = jnp.zeros((MAX_WIN, d_model), F32)

    @pl.when((bi == 0) & (i == 0))
    def _():
        run_ref[...] = jnp.zeros_like(run_ref)

    x = h_ref[...]
    ext_ref[MAX_WIN:MAX_WIN + tile, :] = x
    pos = i * tile + lax.broadcasted_iota(jnp.int32, (tile, 1), 0)
    ys = []
    for g, w in enumerate(POOL_WINDOWS):
        lo, hi = g * gdim, (g + 1) * gdim
        xg = x[:, lo:hi]
        s = xg
        for j in range(1, w):
            s = s + ext_ref[MAX_WIN - j:MAX_WIN - j + tile, lo:hi]
        cnt = jnp.minimum(pos + 1, w).astype(F32)
        u = s / cnt - xg
        ys.append(jnp.dot(u.astype(BF16), pw_ref[g].astype(BF16), preferred_element_type=F32))
    y = jnp.concatenate(ys, axis=-1) * ps_ref[...]
    hn = _layer_norm(alpha * x + y, g_ref[...], b_ref[...])
    h1_ref[...] = hn
    _store_row_tiles(h1t_ref, hn)
    ext_ref[0:MAX_WIN, :] = x[tile - MAX_WIN:tile, :]
    _route(hn, rw_ref, rb_ref, run_ref, ti_ref, gate_ref, rank_ref, cnt_ref)


def _store_row_tiles(out_ref, x):
    for c in range(out_ref.shape[1]):
        out_ref[:, c, :] = x[:, c * LANES:(c + 1) * LANES]


def _load_row_tiles(ref):
    return jnp.concatenate([ref[:, c, :] for c in range(ref.shape[1])], axis=-1)


def _route_out_shapes(tp, n_exp):
    return [jax.ShapeDtypeStruct((tp, TOP_K), jnp.int32),
            jax.ShapeDtypeStruct((tp, TOP_K), F32),
            jax.ShapeDtypeStruct((tp, TOP_K), jnp.int32),
            jax.ShapeDtypeStruct((1, n_exp), F32)]


def _route_out_specs(nt, tile, n_exp):
    row = lambda b, i: (b * nt + i, 0)
    return [pl.BlockSpec((tile, TOP_K), row),
            pl.BlockSpec((tile, TOP_K), row),
            pl.BlockSpec((tile, TOP_K), row),
            pl.BlockSpec((1, n_exp), lambda b, i: (0, 0))]


def _row_tile_spec(nt, tile, d):
    return pl.BlockSpec((tile, d // LANES, LANES), lambda b, i: (b * nt + i, 0, 0))


def _const_spec(shape):
    return pl.BlockSpec(shape, lambda b, i: (0,) * len(shape))


def _pool_call(h, pool_w, pool_scale, ln_g, ln_b, router_w, router_b, *, bsz, lp, alpha):
    tp, d = h.shape
    tile = SEQ_TILE
    nt = lp // tile
    n_exp = router_w.shape[1]
    groups, gdim, _ = pool_w.shape
    row = lambda b, i: (b * nt + i, 0)
    return pl.pallas_call(
        functools.partial(_pool_kernel, tile=tile, alpha=alpha),
        grid=(bsz, nt),
        in_specs=[pl.BlockSpec((tile, d), row),
                  _const_spec((groups, gdim, gdim)),
                  _const_spec((1, d)), _const_spec((1, d)), _const_spec((1, d)),
                  _const_spec((d, n_exp)), _const_spec((1, n_exp))],
        out_specs=[pl.BlockSpec((tile, d), row), _row_tile_spec(nt, tile, d)] + _route_out_specs(nt, tile, n_exp),
        out_shape=[jax.ShapeDtypeStruct((tp, d), F32), jax.ShapeDtypeStruct((tp, d // LANES, LANES), F32)]
        + _route_out_shapes(tp, n_exp),
        scratch_shapes=[pltpu.VMEM((MAX_WIN + tile, d), F32), pltpu.VMEM((1, n_exp), F32)],
        compiler_params=_params(("arbitrary", "arbitrary")),
    )(h, pool_w, pool_scale.reshape(1, d), ln_g.reshape(1, d), ln_b.reshape(1, d),
      router_w, router_b.reshape(1, n_exp))


def _plan(counts, top_i, rank, n_blocks):
    tp = top_i.shape[0]
    n_exp = counts.shape[1]
    n_rows = n_blocks * EXPERT_ROWS
    cnt = counts.reshape(n_exp).astype(jnp.int32)
    padded = (cnt + EXPERT_ROWS - 1) // EXPERT_ROWS * EXPERT_ROWS
    pad_ends = jnp.cumsum(padded)
    pad_starts = pad_ends - padded
    hot = top_i[..., None] == jnp.arange(n_exp, dtype=jnp.int32)
    dest = (jnp.sum(jnp.where(hot, pad_starts, 0), axis=-1) + rank).reshape(-1)
    starts = jnp.arange(n_blocks, dtype=jnp.int32) * EXPERT_ROWS
    block_e = jnp.sum((starts[:, None] >= pad_ends[None, :]).astype(jnp.int32), axis=1)
    block_e = jnp.minimum(block_e, n_exp - 1)
    n_used = (pad_ends[-1] // EXPERT_ROWS).reshape(1)

    copy = jnp.arange(tp * TOP_K, dtype=jnp.int32)
    row_copy = jnp.full((n_rows,), -1, jnp.int32).at[dest].set(copy, unique_indices=True)
    valid = row_copy >= 0
    tok, k = row_copy // TOP_K, row_copy % TOP_K
    scrap = tp * TOP_K + jnp.cumsum(jnp.logical_not(valid).astype(jnp.int32)) - 1
    row_tok = jnp.where(valid, tok, 0)
    row_dst = jnp.where(valid, k * tp + tok, scrap)
    first = n_rows + jnp.arange(EXPERT_ROWS, dtype=jnp.int32) % (EXPERT_ROWS // 2)
    dst_ext = jnp.concatenate([first, row_dst])
    return (block_e, n_used, row_tok.reshape(n_blocks, 1, EXPERT_ROWS),
            dst_ext.reshape(n_blocks + 1, 1, EXPERT_ROWS))


def _ffn_kernel(be_ref, nu_ref, tok_ref, tokn_ref, dstp_ref, dst_ref, h_ref, w1_ref, b1_ref, w2_ref, b2_ref,
                y_ref, xa_ref, xb_ref, ya_ref, yb_ref, w1b_ref, w2b_ref, gsem, ssem):
    i = pl.program_id(0)
    half = xa_ref.shape[0]
    f = w2b_ref.shape[0]

    def gather(idx_ref, base, buf_ref, sem):
        for r in range(half):
            pltpu.make_async_copy(h_ref.at[idx_ref[0, 0, base + r]], buf_ref.at[r], sem).start(priority=r % 2)

    def scatter(idx_ref, base, buf_ref, sem):
        for r in range(half):
            pltpu.make_async_copy(buf_ref.at[r], y_ref.at[idx_ref[0, 0, base + r]], sem).start(priority=r % 2)

    def wait_gather(buf_ref, sem):
        pltpu.make_async_copy(h_ref.at[pl.ds(0, half)], buf_ref, sem).wait()

    def wait_scatter(buf_ref, sem):
        pltpu.make_async_copy(buf_ref, y_ref.at[pl.ds(0, half)], sem).wait()

    def ffn(x_ref, out_ref):
        hid = jnp.dot(_load_row_tiles(x_ref).astype(BF16), w1b_ref[...], preferred_element_type=F32) + b1_ref[0]
        gate = jnp.minimum(hid[:, :f], SWIGLU_LIMIT)
        up = jnp.clip(hid[:, f:], -SWIGLU_LIMIT, SWIGLU_LIMIT)
        act = gate * jax.nn.sigmoid(SWIGLU_ALPHA * gate) * (up + 1.0)
        _store_row_tiles(out_ref, jnp.dot(act.astype(BF16), w2b_ref[...], preferred_element_type=F32) + b2_ref[0])

    @pl.when(i < nu_ref[0])
    def _():
        @pl.when(i == 0)
        def _():
            yb_ref[...] = jnp.zeros_like(yb_ref)
            gather(tok_ref, 0, xa_ref, gsem.at[0])

        @pl.when((i == 0) | (be_ref[i] != be_ref[jnp.maximum(i - 1, 0)]))
        def _():
            w1b_ref[...] = w1_ref[0, 0].astype(BF16)
            w2b_ref[...] = w2_ref[0, 0].astype(BF16)

        @pl.when(i >= 1)
        def _():
            wait_scatter(ya_ref, ssem.at[0])

        wait_gather(xa_ref, gsem.at[0])
        gather(tok_ref, half, xb_ref, gsem.at[1])
        scatter(dstp_ref, half, yb_ref, ssem.at[1])
        ffn(xa_ref, ya_ref)

        wait_gather(xb_ref, gsem.at[1])
        wait_scatter(yb_ref, ssem.at[1])
        gather(tokn_ref, 0, xa_ref, gsem.at[0])
        scatter(dst_ref, 0, ya_ref, ssem.at[0])
        ffn(xb_ref, yb_ref)

        @pl.when(i == nu_ref[0] - 1)
        def _():
            scatter(dst_ref, half, yb_ref, ssem.at[1])
            wait_scatter(ya_ref, ssem.at[0])
            wait_scatter(yb_ref, ssem.at[1])
            wait_gather(xa_ref, gsem.at[0])

    @pl.when(i >= nu_ref[0])
    def _():
        ya_ref[...] = jnp.zeros_like(ya_ref)
        first = dst_ref[0, 0, 0]
        copies = [pltpu.make_async_copy(ya_ref, y_ref.at[pl.ds(first + j * half, half)], ssem.at[j])
                  for j in range(2)]
        for cp in copies:
            cp.start()
        for cp in copies:
            cp.wait()


def _ffn_call(block_e, n_used, row_tok, dst_ext, h_tiles, layer, w1, b1, w2, b2):
    tp, chunks, _ = h_tiles.shape
    d = chunks * LANES
    depth, n_exp, _, f2 = w1.shape
    f = w2.shape[2]
    nb = row_tok.shape[0]
    rb = EXPERT_ROWS
    half = rb // 2
    n_out = nb * rb + half

    def blk(i, be, nu):
        return jnp.minimum(i, nu[0] - 1)

    def idx_spec(index_map):
        return pl.BlockSpec((1, 1, rb), index_map, memory_space=pltpu.SMEM)

    grid_spec = pltpu.PrefetchScalarGridSpec(
        num_scalar_prefetch=2,
        grid=(nb,),
        in_specs=[idx_spec(lambda i, be, nu: (i, 0, 0)),
                  idx_spec(lambda i, be, nu: (jnp.minimum(i + 1, nb - 1), 0, 0)),
                  idx_spec(lambda i, be, nu: (i, 0, 0)),
                  idx_spec(lambda i, be, nu: (i + 1, 0, 0)),
                  pl.BlockSpec(memory_space=pl.ANY),
                  pl.BlockSpec((1, 1, d, f2), lambda i, be, nu: (layer, be[blk(i, be, nu)], 0, 0)),
                  pl.BlockSpec((1, 1, f2), lambda i, be, nu: (layer * n_exp + be[blk(i, be, nu)], 0, 0)),
                  pl.BlockSpec((1, 1, f, d), lambda i, be, nu: (layer, be[blk(i, be, nu)], 0, 0)),
                  pl.BlockSpec((1, 1, d), lambda i, be, nu: (layer * n_exp + be[blk(i, be, nu)], 0, 0))],
        out_specs=pl.BlockSpec(memory_space=pl.ANY),
        scratch_shapes=[pltpu.VMEM((half, chunks, LANES), F32)] * 4 + [
            pltpu.VMEM((d, f2), BF16), pltpu.VMEM((f, d), BF16),
            pltpu.SemaphoreType.DMA((2,)), pltpu.SemaphoreType.DMA((2,))],
    )
    return pl.pallas_call(
        _ffn_kernel,
        grid_spec=grid_spec,
        out_shape=jax.ShapeDtypeStruct((n_out, chunks, LANES), F32),
        compiler_params=_params(("arbitrary",)),
    )(block_e, n_used, row_tok, row_tok, dst_ext, dst_ext, h_tiles,
      w1, b1.reshape(depth * n_exp, 1, f2), w2, b2.reshape(depth * n_exp, 1, d))


def _combine_kernel(gate_ref, h_ref, g_ref, b_ref, *refs, alpha):
    y_refs, out_ref = refs[:TOP_K], refs[TOP_K]
    gates = gate_ref[...]
    ffn = gates[:, 0:1] * _load_row_tiles(y_refs[0])
    for k in range(1, TOP_K):
        ffn = ffn + gates[:, k:k + 1] * _load_row_tiles(y_refs[k])
    out_ref[...] = _layer_norm(alpha * h_ref[...] + ffn, g_ref[...], b_ref[...])


def _combine_call(gates, h, y, ln_g, ln_b, *, alpha):
    tp, d = h.shape
    tile = ROUTE_TILE
    nt = tp // tile
    y_specs = [pl.BlockSpec((tile,) + y.shape[1:], functools.partial(lambda k, i: (k * nt + i, 0, 0), k))
               for k in range(TOP_K)]
    return pl.pallas_call(
        functools.partial(_combine_kernel, alpha=alpha),
        grid=(nt,),
        in_specs=[pl.BlockSpec((tile, TOP_K), lambda i: (i, 0)),
                  pl.BlockSpec((tile, d), lambda i: (i, 0)),
                  pl.BlockSpec((1, d), lambda i: (0, 0)),
                  pl.BlockSpec((1, d), lambda i: (0, 0))] + y_specs,
        out_specs=pl.BlockSpec((tile, d), lambda i: (i, 0)),
        out_shape=jax.ShapeDtypeStruct((tp, d), F32),
        compiler_params=_params(("arbitrary",)),
    )(gates, h, ln_g.reshape(1, d), ln_b.reshape(1, d), *([y] * TOP_K))


def _moe(h, h_tiles, top_i, gates, rank, counts, layer, w1, b1, w2, b2, ln_g, ln_b, *, alpha):
    tp = h.shape[0]
    n_exp = w1.shape[1]
    n_blocks = tp * TOP_K // EXPERT_ROWS + n_exp
    block_e, n_used, row_tok, dst_ext = _plan(counts, top_i, rank, n_blocks)
    y = _ffn_call(block_e, n_used, row_tok, dst_ext, h_tiles, layer, w1, b1, w2, b2)
    return _combine_call(gates, h, y, ln_g, ln_b, alpha=alpha)


def _log_sigmoid(x):
    return jnp.minimum(x, 0.0) - jnp.log(1.0 + jnp.exp(-jnp.abs(x)))


def _split3(c):
    hi = c.astype(BF16)
    r1 = c - hi.astype(F32)
    mid = r1.astype(BF16)
    lo = (r1 - mid.astype(F32)).astype(BF16)
    return hi, mid, lo


def _proj_kernel(h_ref, wqt_ref, wk_ref, wvt_ref, wf_ref, wft_ref, bf_ref, bft_ref,
                 selq_ref, selk_ref, oneq_ref, onek_ref, onev_ref,
                 qt_ref, kx_ref, vt_ref, carry_ref, carryt_ref, *, tile, n_heads):
    i = pl.program_id(1)

    @pl.when(i == 0)
    def _():
        carry_ref[...] = jnp.zeros_like(carry_ref)
        carryt_ref[...] = jnp.zeros_like(carryt_ref)

    x = h_ref[...]
    xb = x.astype(BF16)
    nt_dims = (((1,), (1,)), ((), ()))
    lf = _log_sigmoid(jnp.dot(x, wf_ref[...], precision=HIGHEST, preferred_element_type=F32) + bf_ref[...])
    lft = _log_sigmoid(lax.dot_general(wft_ref[...], x, nt_dims, precision=HIGHEST,
                                       preferred_element_type=F32) + bft_ref[...])
    r = lax.broadcasted_iota(jnp.int32, (tile, tile), 0)
    c = lax.broadcasted_iota(jnp.int32, (tile, tile), 1)
    cs = jnp.dot((c <= r).astype(F32), lf, precision=HIGHEST, preferred_element_type=F32) + carry_ref[...]
    cst = jnp.dot(lft, (r <= c).astype(F32), precision=HIGHEST, preferred_element_type=F32) + carryt_ref[...]
    carry_ref[...] = cs[tile - 1:tile, :]
    carryt_ref[...] = cst[:, tile - 1:tile]

    qt = lax.dot_general(wqt_ref[...], xb, nt_dims, preferred_element_type=F32) + oneq_ref[...]
    for p, part in enumerate(_split3(cst * LOG2E)):
        qt = qt + jnp.dot(selq_ref[p], part, preferred_element_type=F32)
    kx = jnp.dot(xb, wk_ref[...], preferred_element_type=F32) + onek_ref[...]
    for p, part in enumerate(_split3(-cs * LOG2E)):
        kx = kx + jnp.dot(part, selk_ref[p], preferred_element_type=F32)
    vt = lax.dot_general(wvt_ref[...], xb, nt_dims, preferred_element_type=F32) + onev_ref[...]
    for hd in range(n_heads):
        qt_ref[0, hd, 0] = qt[hd * LANES:(hd + 1) * LANES, :].astype(BF16)
        kx_ref[0, hd, 0] = kx[:, hd * LANES:(hd + 1) * LANES].astype(BF16)
        vt_ref[0, hd, 0] = vt[hd * LANES:(hd + 1) * LANES, :].astype(BF16)


def _attn_weights(w_in, b_f, n_heads, head_dim):
    d = w_in.shape[0]
    scale = head_dim ** -0.5 * LOG2E
    hw = n_heads * LANES

    def pad_heads(w):
        w = w.reshape(d, n_heads, head_dim)
        return jnp.pad(w, ((0, 0), (0, 0), (0, LANES - head_dim))).reshape(d, hw)

    wqt = pad_heads(w_in[:, :d] * scale).T.astype(BF16)
    wk = pad_heads(w_in[:, d:2 * d]).astype(BF16)
    wvt = pad_heads(w_in[:, 2 * d:3 * d]).T.astype(BF16)
    wf = w_in[:, 3 * d:]
    selq = np.zeros((3, hw, n_heads), np.float32)
    selk = np.zeros((3, n_heads, hw), np.float32)
    oneq = np.zeros((hw, 1), np.float32)
    onek = np.zeros((1, hw), np.float32)
    onev = np.zeros((hw, 1), np.float32)
    for h in range(n_heads):
        base = h * LANES + head_dim
        for p in range(3):
            selq[p, base + p, h] = 1.0
            selk[p, h, base + 3 + p] = 1.0
            oneq[base + 3 + p, 0] = 1.0
            onek[0, base + p] = 1.0
        onev[base, 0] = 1.0
    return (wqt, wk, wvt, wf, wf.T, b_f.reshape(1, n_heads), b_f.reshape(n_heads, 1),
            jnp.asarray(selq, BF16), jnp.asarray(selk, BF16),
            jnp.asarray(oneq), jnp.asarray(onek), jnp.asarray(onev))


def _proj_call(h, weights, *, bsz, lp, n_heads):
    tp, d = h.shape
    tile = SEQ_TILE
    nt = lp // tile
    hw = n_heads * LANES
    row = lambda b, i: (b * nt + i, 0)
    in_specs = [pl.BlockSpec((tile, d), row)] + [_const_spec(w.shape) for w in weights]
    t_spec = pl.BlockSpec((1, n_heads, 1, LANES, tile), lambda b, i: (b, 0, i, 0, 0))
    k_spec = pl.BlockSpec((1, n_heads, 1, tile, LANES), lambda b, i: (b, 0, i, 0, 0))
    return pl.pallas_call(
        functools.partial(_proj_kernel, tile=tile, n_heads=n_heads),
        grid=(bsz, nt),
        in_specs=in_specs,
        out_specs=[t_spec, k_spec, t_spec],
        out_shape=[jax.ShapeDtypeStruct((bsz, n_heads, nt, LANES, tile), BF16),
                   jax.ShapeDtypeStruct((bsz, n_heads, nt, tile, LANES), BF16),
                   jax.ShapeDtypeStruct((bsz, n_heads, nt, LANES, tile), BF16)],
        scratch_shapes=[pltpu.VMEM((1, n_heads), F32), pltpu.VMEM((n_heads, 1), F32)],
        compiler_params=_params(("arbitrary", "arbitrary")),
    )(h, *weights)


def _attn_kernel(qt_ref, kx_ref, vt_ref, o_ref, s0_ref, s1_ref, s2_ref, p0_ref, p1_ref, p2_ref,
                 a0_ref, a1_ref, a2_ref, c0_ref, c1_ref, c2_ref, m_ref, acc_ref, *, nt, tile, head_dim):
    s_refs, p_refs, a_refs = (s0_ref, s1_ref, s2_ref), (p0_ref, p1_ref, p2_ref), (a0_ref, a1_ref, a2_ref)
    c_refs = (c0_ref, c1_ref, c2_ref)
    key = lax.broadcasted_iota(jnp.int32, (tile, tile), 0)
    qry = lax.broadcasted_iota(jnp.int32, (tile, tile), 1)

    def logits(slot, qi, kj):
        s = jnp.dot(kx_ref[0, 0, kj], qt_ref[0, 0, qi], preferred_element_type=F32)
        s_refs[slot][...] = s
        c_refs[slot][...] = jnp.max(s, axis=0, keepdims=True)

    def value_update(slot, kj, out_qi=None):
        acc = a_refs[slot][...] * acc_ref[...] + jnp.dot(
            vt_ref[0, 0, kj], p_refs[slot][...], preferred_element_type=F32)
        acc_ref[...] = acc
        if out_qi is not None:
            o_ref[0, 0, out_qi] = (acc[:head_dim, :] * (1.0 / acc[head_dim:head_dim + 1, :])).astype(BF16)

    def substep(slot, qi, kj, *, diag, write_out):
        if diag:
            nqi, nkj = qi + 1, jnp.int32(1)
        else:
            stay = qi - kj >= 2
            nqi, nkj = jnp.where(stay, qi, qi + 1), jnp.where(stay, kj + 2, 0)
        logits((slot + 2) % ATTN_DEPTH, jnp.minimum(nqi, nt - 1), nkj)
        pkj = jnp.where(kj >= 2, kj - 2, jnp.maximum(qi - 1 - jnp.where(kj == 0, 1, 0), 0))
        out_qi = jnp.where(kj == 1, qi - 1, qi) if write_out else None
        value_update((slot + 1) % ATTN_DEPTH, pkj, out_qi)
        m_old = m_ref[...]
        if diag:
            s = jnp.where(key <= qry, s_refs[slot][...], MASK_VALUE)
            m_new = jnp.maximum(m_old, jnp.max(s, axis=0, keepdims=True))
            p_refs[slot][...] = jnp.exp2(s - m_new).astype(BF16)
            m_ref[...] = jnp.full_like(m_old, MASK_VALUE)
        else:
            m_new = jnp.maximum(m_old, c_refs[slot][...])
            p_refs[slot][...] = jnp.exp2(s_refs[slot][...] - m_new).astype(BF16)
            m_ref[...] = m_new
        a_refs[slot][...] = jnp.exp2(m_old - m_new)

    def q_tile(slot, qi, n_loops, rem):
        def trip(i, carry):
            for j in range(ATTN_DEPTH):
                substep((slot + j) % ATTN_DEPTH, qi, ATTN_DEPTH * i + j, diag=False, write_out=j == 1)
            return carry

        lax.fori_loop(0, n_loops, trip, 0)
        for j in range(rem):
            substep((slot + j) % ATTN_DEPTH, qi, ATTN_DEPTH * n_loops + j, diag=False, write_out=j == 1)
        substep((slot + rem) % ATTN_DEPTH, qi, qi, diag=True, write_out=rem == 1)
        return (slot + rem + 1) % ATTN_DEPTH

    m_ref[...] = jnp.full_like(m_ref, MASK_VALUE)
    acc_ref[...] = jnp.ones_like(acc_ref)
    for slot in range(1, ATTN_DEPTH):
        p_refs[slot][...] = jnp.zeros_like(p_refs[slot])
        a_refs[slot][...] = jnp.ones_like(a_refs[slot])
    logits(0, 0, 0)
    logits(1, 1, 0)

    def group(g, carry):
        slot = 0
        for r in range(ATTN_DEPTH):
            slot = q_tile(slot, ATTN_DEPTH * g + r, g, r)
        assert slot == 0
        return carry

    lax.fori_loop(0, nt // ATTN_DEPTH, group, 0)
    slot = 0
    for qi in range(nt // ATTN_DEPTH * ATTN_DEPTH, nt):
        slot = q_tile(slot, jnp.int32(qi), jnp.int32(qi // ATTN_DEPTH), qi % ATTN_DEPTH)
    value_update((slot + 1) % ATTN_DEPTH, nt - 2)
    value_update((slot + 2) % ATTN_DEPTH, nt - 1, nt - 1)


def _attn_call(qt, kx, vt, *, head_dim):
    bsz, n_heads, nt, _, tile = qt.shape
    return pl.pallas_call(
        functools.partial(_attn_kernel, nt=nt, tile=tile, head_dim=head_dim),
        grid=(bsz, n_heads),
        in_specs=[pl.BlockSpec((1, 1, nt, LANES, tile), lambda b, h: (b, h, 0, 0, 0)),
                  pl.BlockSpec((1, 1, nt, tile, LANES), lambda b, h: (b, h, 0, 0, 0)),
                  pl.BlockSpec((1, 1, nt, LANES, tile), lambda b, h: (b, h, 0, 0, 0))],
        out_specs=pl.BlockSpec((1, 1, nt, head_dim, tile), lambda b, h: (b, h, 0, 0, 0)),
        out_shape=jax.ShapeDtypeStruct((bsz, n_heads, nt, head_dim, tile), BF16),
        scratch_shapes=([pltpu.VMEM((tile, tile), F32)] * ATTN_DEPTH + [pltpu.VMEM((tile, tile), BF16)] * ATTN_DEPTH
                        + [pltpu.VMEM((1, tile), F32)] * (2 * ATTN_DEPTH)
                        + [pltpu.VMEM((1, tile), F32), pltpu.VMEM((LANES, tile), F32)]),
        compiler_params=_params(("arbitrary", "arbitrary")),
    )(qt, kx, vt)


def _oproj_kernel(o_ref, wo_ref, h_ref, g_ref, b_ref, rw_ref, rb_ref,
                  hn_ref, hnt_ref, ti_ref, gate_ref, rank_ref, cnt_ref, run_ref, *, alpha):
    bi = pl.program_id(0)
    i = pl.program_id(1)

    @pl.when((bi == 0) & (i == 0))
    def _():
        run_ref[...] = jnp.zeros_like(run_ref)

    n_heads, _, head_dim, tile = o_ref.shape[1:]
    o_t = o_ref[0].reshape(n_heads * head_dim, tile)
    att = lax.dot_general(o_t, wo_ref[...], (((0,), (0,)), ((), ())), preferred_element_type=F32)
    hn = _layer_norm(alpha * h_ref[...] + att, g_ref[...], b_ref[...])
    hn_ref[...] = hn
    _store_row_tiles(hnt_ref, hn)
    _route(hn, rw_ref, rb_ref, run_ref, ti_ref, gate_ref, rank_ref, cnt_ref)


def _oproj_call(o, w_out, h, ln_g, ln_b, router_w, router_b, *, bsz, lp, alpha):
    tp, d = h.shape
    tile = SEQ_TILE
    nt = lp // tile
    n_exp = router_w.shape[1]
    row = lambda b, i: (b * nt + i, 0)
    return pl.pallas_call(
        functools.partial(_oproj_kernel, alpha=alpha),
        grid=(bsz, nt),
        in_specs=[pl.BlockSpec((1,) + o.shape[1:2] + (1,) + o.shape[3:], lambda b, i: (b, 0, i, 0, 0)),
                  _const_spec((d, d)),
                  pl.BlockSpec((tile, d), row),
                  _const_spec((1, d)), _const_spec((1, d)),
                  _const_spec((d, n_exp)), _const_spec((1, n_exp))],
        out_specs=[pl.BlockSpec((tile, d), row), _row_tile_spec(nt, tile, d)] + _route_out_specs(nt, tile, n_exp),
        out_shape=[jax.ShapeDtypeStruct((tp, d), F32), jax.ShapeDtypeStruct((tp, d // LANES, LANES), F32)]
        + _route_out_shapes(tp, n_exp),
        scratch_shapes=[pltpu.VMEM((1, n_exp), F32)],
        compiler_params=_params(("arbitrary", "arbitrary")),
    )(o, w_out.astype(BF16), h, ln_g.reshape(1, d), ln_b.reshape(1, d),
      router_w, router_b.reshape(1, n_exp))


def kernel(x, meta_tokens, pool_w, pool_scale, attn_w_in, attn_b_f, attn_w_out,
           ln_g, ln_b, router_w, router_b, w1, b1, w2, b2):
    bsz, seq, d = x.shape
    n_meta = meta_tokens.shape[0]
    depth = ln_g.shape[0]
    n_heads = attn_b_f.shape[-1]
    head_dim = d // n_heads
    alpha = float((2 * depth) ** 0.25)
    length = n_meta + seq
    lp = -(-length // SEQ_TILE) * SEQ_TILE
    assert d % (len(POOL_WINDOWS) * LANES) == 0 and head_dim + 6 <= LANES and depth == 2

    meta = jnp.broadcast_to(meta_tokens[None], (bsz, n_meta, d))
    h = jnp.concatenate([meta, x, jnp.zeros((bsz, lp - length, d), x.dtype)], axis=1)
    h = h.reshape(bsz * lp, d)

    h, h_tiles, top_i, gates, rank, counts = _pool_call(
        h, pool_w[0], pool_scale[0], ln_g[0, 0], ln_b[0, 0], router_w[0], router_b[0],
        bsz=bsz, lp=lp, alpha=alpha)
    h = _moe(h, h_tiles, top_i, gates, rank, counts, 0, w1, b1, w2, b2, ln_g[0, 1], ln_b[0, 1], alpha=alpha)

    weights = _attn_weights(attn_w_in[0], attn_b_f[0], n_heads, head_dim)
    qt, kx, vt = _proj_call(h, weights, bsz=bsz, lp=lp, n_heads=n_heads)
    o = _attn_call(qt, kx, vt, head_dim=head_dim)
    h, h_tiles, top_i, gates, rank, counts = _oproj_call(
        o, attn_w_out[0], h, ln_g[1, 0], ln_b[1, 0], router_w[1], router_b[1],
        bsz=bsz, lp=lp, alpha=alpha)
    h = _moe(h, h_tiles, top_i, gates, rank, counts, 1, w1, b1, w2, b2, ln_g[1, 1], ln_b[1, 1], alpha=alpha)

    return h.reshape(bsz, lp, d)[:, n_meta:length]
```

```python
import functools

import numpy as np
import jax
import jax.numpy as jnp
from jax import lax
from jax.experimental import pallas as pl
from jax.experimental.pallas import tpu as pltpu

POOL_WINDOWS = (2, 4, 8, 16)
MAX_WIN = max(POOL_WINDOWS)
TOP_K = 4
SWIGLU_LIMIT = 7.0
SWIGLU_ALPHA = 1.702
LN_EPS = 1e-5
MASK_VALUE = -1e30

LANES = 128
SEQ_TILE = 512
ROUTE_TILE = 512
EXPERT_ROWS = 512
VMEM_LIMIT = 56 * 1024 * 1024
LOG2E = 1.4426950408889634
ATTN_DEPTH = 3

F32 = jnp.float32
BF16 = jnp.bfloat16


def _params(sem, vmem=VMEM_LIMIT):
    return pltpu.CompilerParams(dimension_semantics=sem, vmem_limit_bytes=vmem)


def _dot3(a, b):
    a_hi = a.astype(BF16)
    a_lo = (a - a_hi.astype(F32)).astype(BF16)
    b_hi = b.astype(BF16)
    b_lo = (b - b_hi.astype(F32)).astype(BF16)
    dot = functools.partial(jnp.dot, preferred_element_type=F32)
    return dot(a_hi, b_hi) + dot(a_hi, b_lo) + dot(a_lo, b_hi)


def _layer_norm(z, g, b):
    mu = jnp.mean(z, axis=-1, keepdims=True)
    d = z - mu
    var = jnp.mean(d * d, axis=-1, keepdims=True)
    return d * lax.rsqrt(var + LN_EPS) * g + b


def _route(hn, rw_ref, rb_ref, run_ref, ti_ref, gate_ref, rank_ref, cnt_ref):
    rows = hn.shape[0]
    logits = _dot3(hn, rw_ref[...]) + rb_ref[...]
    n_exp = logits.shape[1]
    lane = lax.broadcasted_iota(jnp.int32, (rows, n_exp), 1)
    cur = logits
    vals, idxs, hots = [], [], []
    for _ in range(TOP_K):
        m = jnp.max(cur, axis=-1, keepdims=True)
        idx = jnp.min(jnp.where(cur == m, lane, n_exp), axis=-1, keepdims=True)
        hot = lane == idx
        vals.append(m)
        idxs.append(idx)
        hots.append(hot)
        cur = jnp.where(hot, -jnp.inf, cur)
    exps = [jnp.exp(v - vals[0]) for v in vals]
    denom = exps[0]
    for e in exps[1:]:
        denom = denom + e
    gates = [e / denom for e in exps]
    sel = hots[0].astype(F32)
    for hot in hots[1:]:
        sel = sel + hot.astype(F32)
    r = lax.broadcasted_iota(jnp.int32, (rows, rows), 0)
    c = lax.broadcasted_iota(jnp.int32, (rows, rows), 1)
    tri = (c < r).astype(BF16)
    base = run_ref[...] + jnp.dot(tri, sel.astype(BF16), preferred_element_type=F32)
    ranks = [jnp.sum(jnp.where(hot, base, 0.0), axis=-1, keepdims=True) for hot in hots]
    run_ref[...] = run_ref[...] + jnp.sum(sel, axis=0, keepdims=True)
    cnt_ref[...] = run_ref[...]
    lane_k = lax.broadcasted_iota(jnp.int32, (rows, TOP_K), 1)

    def pack(cols):
        out = jnp.broadcast_to(cols[TOP_K - 1], (rows, TOP_K))
        for k in range(TOP_K - 2, -1, -1):
            out = jnp.where(lane_k == k, cols[k], out)
        return out

    ti_ref[...] = pack(idxs)
    gate_ref[...] = pack(gates)
    rank_ref[...] = pack(ranks).astype(jnp.int32)


def _pool_kernel(h_ref, pw_ref, ps_ref, g_ref, b_ref, rw_ref, rb_ref,
                 h1_ref, h1t_ref, ti_ref, gate_ref, rank_ref, cnt_ref,
                 ext_ref, run_ref, *, tile, alpha):
    bi = pl.program_id(0)
    i = pl.program_id(1)
    d_model = h_ref.shape[1]
    gdim = d_model // len(POOL_WINDOWS)

    @pl.when(i == 0)
    def _():
        ext_ref[0:MAX_WIN, :] = jnp.zeros((MAX_WIN, d_model), F32)

    @pl.when((bi == 0) & (i == 0))
    def _():
        run_ref[...] = jnp.zeros_like(run_ref)

    x = h_ref[...]
    ext_ref[MAX_WIN:MAX_WIN + tile, :] = x
    pos = i * tile + lax.broadcasted_iota(jnp.int32, (tile, 1), 0)
    ys = []
    for g, w in enumerate(POOL_WINDOWS):
        lo, hi = g * gdim, (g + 1) * gdim
        xg = x[:, lo:hi]
        s = xg
        for j in range(1, w):
            s = s + ext_ref[MAX_WIN - j:MAX_WIN - j + tile, lo:hi]
        cnt = jnp.minimum(pos + 1, w).astype(F32)
        u = s / cnt - xg
        ys.append(jnp.dot(u.astype(BF16), pw_ref[g].astype(BF16), preferred_element_type=F32))
    y = jnp.concatenate(ys, axis=-1) * ps_ref[...]
    hn = _layer_norm(alpha * x + y, g_ref[...], b_ref[...])
    h1_ref[...] = hn
    _store_row_tiles(h1t_ref, hn)
    ext_ref[0:MAX_WIN, :] = x[tile - MAX_WIN:tile, :]
    _route(hn, rw_ref, rb_ref, run_ref, ti_ref, gate_ref, rank_ref, cnt_ref)


def _store_row_tiles(out_ref, x):
    for c in range(out_ref.shape[1]):
        out_ref[:, c, :] = x[:, c * LANES:(c + 1) * LANES]


def _load_row_tiles(ref):
    return jnp.concatenate([ref[:, c, :] for c in range(ref.shape[1])], axis=-1)


def _route_out_shapes(tp, n_exp):
    return [jax.ShapeDtypeStruct((tp, TOP_K), jnp.int32),
            jax.ShapeDtypeStruct((tp, TOP_K), F32),
            jax.ShapeDtypeStruct((tp, TOP_K), jnp.int32),
            jax.ShapeDtypeStruct((1, n_exp), F32)]


def _route_out_specs(nt, tile, n_exp):
    row = lambda b, i: (b * nt + i, 0)
    return [pl.BlockSpec((tile, TOP_K), row),
            pl.BlockSpec((tile, TOP_K), row),
            pl.BlockSpec((tile, TOP_K), row),
            pl.BlockSpec((1, n_exp), lambda b, i: (0, 0))]


def _row_tile_spec(nt, tile, d):
    return pl.BlockSpec((tile, d // LANES, LANES), lambda b, i: (b * nt + i, 0, 0))


def _const_spec(shape):
    return pl.BlockSpec(shape, lambda b, i: (0,) * len(shape))


def _pool_call(h, pool_w, pool_scale, ln_g, ln_b, router_w, router_b, *, bsz, lp, alpha):
    tp, d = h.shape
    tile = SEQ_TILE
    nt = lp // tile
    n_exp = router_w.shape[1]
    groups, gdim, _ = pool_w.shape
    row = lambda b, i: (b * nt + i, 0)
    return pl.pallas_call(
        functools.partial(_pool_kernel, tile=tile, alpha=alpha),
        grid=(bsz, nt),
        in_specs=[pl.BlockSpec((tile, d), row),
                  _const_spec((groups, gdim, gdim)),
                  _const_spec((1, d)), _const_spec((1, d)), _const_spec((1, d)),
                  _const_spec((d, n_exp)), _const_spec((1, n_exp))],
        out_specs=[pl.BlockSpec((tile, d), row), _row_tile_spec(nt, tile, d)] + _route_out_specs(nt, tile, n_exp),
        out_shape=[jax.ShapeDtypeStruct((tp, d), F32), jax.ShapeDtypeStruct((tp, d // LANES, LANES), F32)]
        + _route_out_shapes(tp, n_exp),
        scratch_shapes=[pltpu.VMEM((MAX_WIN + tile, d), F32), pltpu.VMEM((1, n_exp), F32)],
        compiler_params=_params(("arbitrary", "arbitrary")),
    )(h, pool_w, pool_scale.reshape(1, d), ln_g.reshape(1, d), ln_b.reshape(1, d),
      router_w, router_b.reshape(1, n_exp))


def _plan(counts, top_i, rank, n_blocks):
    tp = top_i.shape[0]
    n_exp = counts.shape[1]
    n_rows = n_blocks * EXPERT_ROWS
    cnt = counts.reshape(n_exp).astype(jnp.int32)
    padded = (cnt + EXPERT_ROWS - 1) // EXPERT_ROWS * EXPERT_ROWS
    pad_ends = jnp.cumsum(padded)
    pad_starts = pad_ends - padded
    hot = top_i[..., None] == jnp.arange(n_exp, dtype=jnp.int32)
    dest = (jnp.sum(jnp.where(hot, pad_starts, 0), axis=-1) + rank).reshape(-1)
    starts = jnp.arange(n_blocks, dtype=jnp.int32) * EXPERT_ROWS
    block_e = jnp.sum((starts[:, None] >= pad_ends[None, :]).astype(jnp.int32), axis=1)
    block_e = jnp.minimum(block_e, n_exp - 1)
    n_used = (pad_ends[-1] // EXPERT_ROWS).reshape(1)

    copy = jnp.arange(tp * TOP_K, dtype=jnp.int32)
    row_copy = jnp.full((n_rows,), -1, jnp.int32).at[dest].set(copy, unique_indices=True)
    valid = row_copy >= 0
    tok, k = row_copy // TOP_K, row_copy % TOP_K
    scrap = tp * TOP_K + jnp.cumsum(jnp.logical_not(valid).astype(jnp.int32)) - 1
    row_tok = jnp.where(valid, tok, 0)
    row_dst = jnp.where(valid, k * tp + tok, scrap)
    first = n_rows + jnp.arange(EXPERT_ROWS, dtype=jnp.int32) % (EXPERT_ROWS // 2)
    dst_ext = jnp.concatenate([first, row_dst])
    return (block_e, n_used, row_tok.reshape(n_blocks, 1, EXPERT_ROWS),
            dst_ext.reshape(n_blocks + 1, 1, EXPERT_ROWS))


def _ffn_kernel(be_ref, nu_ref, tok_ref, tokn_ref, dstp_ref, dst_ref, h_ref, w1_ref, b1_ref, w2_ref, b2_ref,
                y_ref, xa_ref, xb_ref, ya_ref, yb_ref, w1b_ref, w2b_ref, gsem, ssem):
    i = pl.program_id(0)
    half = xa_ref.shape[0]
    f = w2b_ref.shape[0]

    def gather(idx_ref, base, buf_ref, sem):
        for r in range(half):
            pltpu.make_async_copy(h_ref.at[idx_ref[0, 0, base + r]], buf_ref.at[r], sem).start()

    def scatter(idx_ref, base, buf_ref, sem):
        for r in range(half):
            pltpu.make_async_copy(buf_ref.at[r], y_ref.at[idx_ref[0, 0, base + r]], sem).start()

    def wait_gather(buf_ref, sem):
        pltpu.make_async_copy(h_ref.at[pl.ds(0, half)], buf_ref, sem).wait()

    def wait_scatter(buf_ref, sem):
        pltpu.make_async_copy(buf_ref, y_ref.at[pl.ds(0, half)], sem).wait()

    def ffn(x_ref, out_ref):
        hid = jnp.dot(_load_row_tiles(x_ref).astype(BF16), w1b_ref[...], preferred_element_type=F32) + b1_ref[0]
        gate = jnp.minimum(hid[:, :f], SWIGLU_LIMIT)
        up = jnp.clip(hid[:, f:], -SWIGLU_LIMIT, SWIGLU_LIMIT)
        act = gate * jax.nn.sigmoid(SWIGLU_ALPHA * gate) * (up + 1.0)
        _store_row_tiles(out_ref, jnp.dot(act.astype(BF16), w2b_ref[...], preferred_element_type=F32) + b2_ref[0])

    @pl.when(i < nu_ref[0])
    def _():
        @pl.when(i == 0)
        def _():
            yb_ref[...] = jnp.zeros_like(yb_ref)
            gather(tok_ref, 0, xa_ref, gsem.at[0])

        @pl.when((i == 0) | (be_ref[i] != be_ref[jnp.maximum(i - 1, 0)]))
        def _():
            w1b_ref[...] = w1_ref[0, 0].astype(BF16)
            w2b_ref[...] = w2_ref[0, 0].astype(BF16)

        @pl.when(i >= 1)
        def _():
            wait_scatter(ya_ref, ssem.at[0])

        wait_gather(xa_ref, gsem.at[0])
        gather(tok_ref, half, xb_ref, gsem.at[1])
        scatter(dstp_ref, half, yb_ref, ssem.at[1])
        ffn(xa_ref, ya_ref)

        wait_gather(xb_ref, gsem.at[1])
        wait_scatter(yb_ref, ssem.at[1])
        gather(tokn_ref, 0, xa_ref, gsem.at[0])
        scatter(dst_ref, 0, ya_ref, ssem.at[0])
        ffn(xb_ref, yb_ref)

        @pl.when(i == nu_ref[0] - 1)
        def _():
            scatter(dst_ref, half, yb_ref, ssem.at[1])
            wait_scatter(ya_ref, ssem.at[0])
            wait_scatter(yb_ref, ssem.at[1])
            wait_gather(xa_ref, gsem.at[0])

    @pl.when(i >= nu_ref[0])
    def _():
        ya_ref[...] = jnp.zeros_like(ya_ref)
        first = dst_ref[0, 0, 0]
        copies = [pltpu.make_async_copy(ya_ref, y_ref.at[pl.ds(first + j * half, half)], ssem.at[j])
                  for j in range(2)]
        for cp in copies:
            cp.start()
        for cp in copies:
            cp.wait()


def _ffn_call(block_e, n_used, row_tok, dst_ext, h_tiles, layer, w1, b1, w2, b2):
    tp, chunks, _ = h_tiles.shape
    d = chunks * LANES
    depth, n_exp, _, f2 = w1.shape
    f = w2.shape[2]
    nb = row_tok.shape[0]
    rb = EXPERT_ROWS
    half = rb // 2
    n_out = nb * rb + half

    def blk(i, be, nu):
        return jnp.minimum(i, nu[0] - 1)

    def idx_spec(index_map):
        return pl.BlockSpec((1, 1, rb), index_map, memory_space=pltpu.SMEM)

    grid_spec = pltpu.PrefetchScalarGridSpec(
        num_scalar_prefetch=2,
        grid=(nb,),
        in_specs=[idx_spec(lambda i, be, nu: (i, 0, 0)),
                  idx_spec(lambda i, be, nu: (jnp.minimum(i + 1, nb - 1), 0, 0)),
                  idx_spec(lambda i, be, nu: (i, 0, 0)),
                  idx_spec(lambda i, be, nu: (i + 1, 0, 0)),
                  pl.BlockSpec(memory_space=pl.ANY),
                  pl.BlockSpec((1, 1, d, f2), lambda i, be, nu: (layer, be[blk(i, be, nu)], 0, 0)),
                  pl.BlockSpec((1, 1, f2), lambda i, be, nu: (layer * n_exp + be[blk(i, be, nu)], 0, 0)),
                  pl.BlockSpec((1, 1, f, d), lambda i, be, nu: (layer, be[blk(i, be, nu)], 0, 0)),
                  pl.BlockSpec((1, 1, d), lambda i, be, nu: (layer * n_exp + be[blk(i, be, nu)], 0, 0))],
        out_specs=pl.BlockSpec(memory_space=pl.ANY),
        scratch_shapes=[pltpu.VMEM((half, chunks, LANES), F32)] * 4 + [
            pltpu.VMEM((d, f2), BF16), pltpu.VMEM((f, d), BF16),
            pltpu.SemaphoreType.DMA((2,)), pltpu.SemaphoreType.DMA((2,))],
    )
    return pl.pallas_call(
        _ffn_kernel,
        grid_spec=grid_spec,
        out_shape=jax.ShapeDtypeStruct((n_out, chunks, LANES), F32),
        compiler_params=_params(("arbitrary",)),
    )(block_e, n_used, row_tok, row_tok, dst_ext, dst_ext, h_tiles,
      w1, b1.reshape(depth * n_exp, 1, f2), w2, b2.reshape(depth * n_exp, 1, d))


def _combine_kernel(gate_ref, h_ref, g_ref, b_ref, *refs, alpha):
    y_refs, out_ref = refs[:TOP_K], refs[TOP_K]
    gates = gate_ref[...]
    ffn = gates[:, 0:1] * _load_row_tiles(y_refs[0])
    for k in range(1, TOP_K):
        ffn = ffn + gates[:, k:k + 1] * _load_row_tiles(y_refs[k])
    out_ref[...] = _layer_norm(alpha * h_ref[...] + ffn, g_ref[...], b_ref[...])


def _combine_call(gates, h, y, ln_g, ln_b, *, alpha):
    tp, d = h.shape
    tile = ROUTE_TILE
    nt = tp // tile
    y_specs = [pl.BlockSpec((tile,) + y.shape[1:], functools.partial(lambda k, i: (k * nt + i, 0, 0), k))
               for k in range(TOP_K)]
    return pl.pallas_call(
        functools.partial(_combine_kernel, alpha=alpha),
        grid=(nt,),
        in_specs=[pl.BlockSpec((tile, TOP_K), lambda i: (i, 0)),
                  pl.BlockSpec((tile, d), lambda i: (i, 0)),
                  pl.BlockSpec((1, d), lambda i: (0, 0)),
                  pl.BlockSpec((1, d), lambda i: (0, 0))] + y_specs,
        out_specs=pl.BlockSpec((tile, d), lambda i: (i, 0)),
        out_shape=jax.ShapeDtypeStruct((tp, d), F32),
        compiler_params=_params(("arbitrary",)),
    )(gates, h, ln_g.reshape(1, d), ln_b.reshape(1, d), *([y] * TOP_K))


def _moe(h, h_tiles, top_i, gates, rank, counts, layer, w1, b1, w2, b2, ln_g, ln_b, *, alpha):
    tp = h.shape[0]
    n_exp = w1.shape[1]
    n_blocks = tp * TOP_K // EXPERT_ROWS + n_exp
    block_e, n_used, row_tok, dst_ext = _plan(counts, top_i, rank, n_blocks)
    y = _ffn_call(block_e, n_used, row_tok, dst_ext, h_tiles, layer, w1, b1, w2, b2)
    return _combine_call(gates, h, y, ln_g, ln_b, alpha=alpha)


def _log_sigmoid(x):
    return jnp.minimum(x, 0.0) - jnp.log(1.0 + jnp.exp(-jnp.abs(x)))


def _split3(c):
    hi = c.astype(BF16)
    r1 = c - hi.astype(F32)
    mid = r1.astype(BF16)
    lo = (r1 - mid.astype(F32)).astype(BF16)
    return hi, mid, lo


def _proj_kernel(h_ref, wqt_ref, wk_ref, wvt_ref, wf_ref, bf_ref, selk_ref, onek_ref,
                 qt_ref, kx_ref, vt_ref, carry_ref, *, tile, n_heads, head_dim):
    i = pl.program_id(1)

    @pl.when(i == 0)
    def _():
        carry_ref[...] = jnp.zeros_like(carry_ref)

    x = h_ref[...]
    xb = x.astype(BF16)
    nt_dims = (((1,), (1,)), ((), ()))
    lf = _log_sigmoid(_dot3(x, wf_ref[...]) + bf_ref[...])
    r = lax.broadcasted_iota(jnp.int32, (tile, tile), 0)
    c = lax.broadcasted_iota(jnp.int32, (tile, tile), 1)
    tril = (c <= r).astype(BF16)
    cs = carry_ref[...]
    for part in _split3(lf):
        cs = cs + jnp.dot(tril, part, preferred_element_type=F32)
    carry_ref[...] = cs[tile - 1:tile, :]
    parts = _split3(cs * LOG2E)
    eye = (r == c).astype(BF16)
    parts_t = [lax.dot_general(p, eye, (((0,), (0,)), ((), ())), preferred_element_type=F32)
               for p in parts]

    kx = jnp.dot(xb, wk_ref[...], preferred_element_type=F32) + onek_ref[...]
    for p, part in enumerate(parts):
        kx = kx + jnp.dot(-part, selk_ref[p], preferred_element_type=F32)
    qt = lax.dot_general(wqt_ref[...], xb, nt_dims, preferred_element_type=F32)
    vt = lax.dot_general(wvt_ref[...], xb, nt_dims, preferred_element_type=F32)
    aug_rows = 16
    row = lax.broadcasted_iota(jnp.int32, (aug_rows, tile), 0)
    v_aug = jnp.where(row == 0, 1.0, 0.0).astype(BF16)
    rest = jnp.zeros((LANES - head_dim - aug_rows, tile), BF16)
    for hd in range(n_heads):
        lo = hd * head_dim
        q_aug = jnp.where(row < 6, 1.0, 0.0)
        for p in range(2, -1, -1):
            q_aug = jnp.where(row == p, parts_t[p][hd:hd + 1, :], q_aug)
        qt_ref[0, hd, 0, 0:head_dim, :] = qt[lo:lo + head_dim, :].astype(BF16)
        qt_ref[0, hd, 0, head_dim:head_dim + aug_rows, :] = q_aug.astype(BF16)
        qt_ref[0, hd, 0, head_dim + aug_rows:, :] = rest
        vt_ref[0, hd, 0, 0:head_dim, :] = vt[lo:lo + head_dim, :].astype(BF16)
        vt_ref[0, hd, 0, head_dim:head_dim + aug_rows, :] = v_aug
        vt_ref[0, hd, 0, head_dim + aug_rows:, :] = rest
        kx_ref[0, hd, 0] = kx[:, hd * LANES:(hd + 1) * LANES].astype(BF16)


def _attn_weights(w_in, b_f, n_heads, head_dim):
    d = w_in.shape[0]
    scale = head_dim ** -0.5 * LOG2E
    hw = n_heads * LANES
    wk = w_in[:, d:2 * d].reshape(d, n_heads, head_dim)
    wk = jnp.pad(wk, ((0, 0), (0, 0), (0, LANES - head_dim))).reshape(d, hw).astype(BF16)
    wqt = (w_in[:, :d] * scale).T.astype(BF16)
    wvt = w_in[:, 2 * d:3 * d].T.astype(BF16)
    selk = np.zeros((3, n_heads, hw), np.float32)
    onek = np.zeros((1, hw), np.float32)
    for h in range(n_heads):
        base = h * LANES + head_dim
        for p in range(3):
            selk[p, h, base + 3 + p] = 1.0
            onek[0, base + p] = 1.0
    return (wqt, wk, wvt, w_in[:, 3 * d:], b_f.reshape(1, n_heads), jnp.asarray(selk, BF16), jnp.asarray(onek))


def _proj_call(h, weights, *, bsz, lp, n_heads, head_dim):
    tp, d = h.shape
    tile = SEQ_TILE
    nt = lp // tile
    row = lambda b, i: (b * nt + i, 0)
    in_specs = [pl.BlockSpec((tile, d), row)] + [_const_spec(w.shape) for w in weights]
    t_spec = pl.BlockSpec((1, n_heads, 1, LANES, tile), lambda b, i: (b, 0, i, 0, 0))
    k_spec = pl.BlockSpec((1, n_heads, 1, tile, LANES), lambda b, i: (b, 0, i, 0, 0))
    return pl.pallas_call(
        functools.partial(_proj_kernel, tile=tile, n_heads=n_heads, head_dim=head_dim),
        grid=(bsz, nt),
        in_specs=in_specs,
        out_specs=[t_spec, k_spec, t_spec],
        out_shape=[jax.ShapeDtypeStruct((bsz, n_heads, nt, LANES, tile), BF16),
                   jax.ShapeDtypeStruct((bsz, n_heads, nt, tile, LANES), BF16),
                   jax.ShapeDtypeStruct((bsz, n_heads, nt, LANES, tile), BF16)],
        scratch_shapes=[pltpu.VMEM((1, n_heads), F32)],
        compiler_params=_params(("arbitrary", "arbitrary")),
    )(h, *weights)


def _attn_kernel(qt_ref, kx_ref, vt_ref, o_ref, s0_ref, s1_ref, s2_ref, p0_ref, p1_ref, p2_ref,
                 a0_ref, a1_ref, a2_ref, c0_ref, c1_ref, c2_ref, m_ref, acc_ref, *, nt, tile, head_dim):
    s_refs, p_refs, a_refs = (s0_ref, s1_ref, s2_ref), (p0_ref, p1_ref, p2_ref), (a0_ref, a1_ref, a2_ref)
    c_refs = (c0_ref, c1_ref, c2_ref)
    key = lax.broadcasted_iota(jnp.int32, (tile, tile), 0)
    qry = lax.broadcasted_iota(jnp.int32, (tile, tile), 1)

    def logits(slot, qi, kj):
        s = jnp.dot(kx_ref[0, 0, kj], qt_ref[0, 0, qi], preferred_element_type=F32)
        s_refs[slot][...] = s
        c_refs[slot][...] = jnp.max(s, axis=0, keepdims=True)

    def value_update(slot, kj, out_qi=None):
        acc = a_refs[slot][...] * acc_ref[...] + jnp.dot(
            vt_ref[0, 0, kj], p_refs[slot][...], preferred_element_type=F32)
        acc_ref[...] = acc
        if out_qi is not None:
            o_ref[0, 0, out_qi] = (acc[:head_dim, :] * (1.0 / acc[head_dim:head_dim + 1, :])).astype(BF16)

    def substep(slot, qi, kj, *, diag, write_out):
        if diag:
            nqi, nkj = qi + 1, jnp.int32(1)
        else:
            stay = qi - kj >= 2
            nqi, nkj = jnp.where(stay, qi, qi + 1), jnp.where(stay, kj + 2, 0)
        logits((slot + 2) % ATTN_DEPTH, jnp.minimum(nqi, nt - 1), nkj)
        pkj = jnp.where(kj >= 2, kj - 2, jnp.maximum(qi - 1 - jnp.where(kj == 0, 1, 0), 0))
        out_qi = jnp.where(kj == 1, qi - 1, qi) if write_out else None
        value_update((slot + 1) % ATTN_DEPTH, pkj, out_qi)
        m_old = m_ref[...]
        if diag:
            s = jnp.where(key <= qry, s_refs[slot][...], MASK_VALUE)
            m_new = jnp.maximum(m_old, jnp.max(s, axis=0, keepdims=True))
            p_refs[slot][...] = jnp.exp2(s - m_new).astype(BF16)
            m_ref[...] = jnp.full_like(m_old, MASK_VALUE)
        else:
            m_new = jnp.maximum(m_old, c_refs[slot][...])
            p_refs[slot][...] = jnp.exp2(s_refs[slot][...] - m_new).astype(BF16)
            m_ref[...] = m_new
        a_refs[slot][...] = jnp.exp2(m_old - m_new)

    def q_tile(slot, qi, n_loops, rem):
        def trip(i, carry):
            for j in range(ATTN_DEPTH):
                substep((slot + j) % ATTN_DEPTH, qi, ATTN_DEPTH * i + j, diag=False, write_out=j == 1)
            return carry

        lax.fori_loop(0, n_loops, trip, 0)
        for j in range(rem):
            substep((slot + j) % ATTN_DEPTH, qi, ATTN_DEPTH * n_loops + j, diag=False, write_out=j == 1)
        substep((slot + rem) % ATTN_DEPTH, qi, qi, diag=True, write_out=rem == 1)
        return (slot + rem + 1) % ATTN_DEPTH

    m_ref[...] = jnp.full_like(m_ref, MASK_VALUE)
    acc_ref[...] = jnp.ones_like(acc_ref)
    for slot in range(1, ATTN_DEPTH):
        p_refs[slot][...] = jnp.zeros_like(p_refs[slot])
        a_refs[slot][...] = jnp.ones_like(a_refs[slot])
    logits(0, 0, 0)
    logits(1, 1, 0)

    def group(g, carry):
        slot = 0
        for r in range(ATTN_DEPTH):
            slot = q_tile(slot, ATTN_DEPTH * g + r, g, r)
        assert slot == 0
        return carry

    lax.fori_loop(0, nt // ATTN_DEPTH, group, 0)
    slot = 0
    for qi in range(nt // ATTN_DEPTH * ATTN_DEPTH, nt):
        slot = q_tile(slot, jnp.int32(qi), jnp.int32(qi // ATTN_DEPTH), qi % ATTN_DEPTH)
    value_update((slot + 1) % ATTN_DEPTH, nt - 2)
    value_update((slot + 2) % ATTN_DEPTH, nt - 1, nt - 1)


def _attn_call(qt, kx, vt, *, head_dim):
    bsz, n_heads, nt, _, tile = qt.shape
    return pl.pallas_call(
        functools.partial(_attn_kernel, nt=nt, tile=tile, head_dim=head_dim),
        grid=(bsz, n_heads),
        in_specs=[pl.BlockSpec((1, 1, nt, LANES, tile), lambda b, h: (b, h, 0, 0, 0)),
                  pl.BlockSpec((1, 1, nt, tile, LANES), lambda b, h: (b, h, 0, 0, 0)),
                  pl.BlockSpec((1, 1, nt, LANES, tile), lambda b, h: (b, h, 0, 0, 0))],
        out_specs=pl.BlockSpec((1, 1, nt, head_dim, tile), lambda b, h: (b, h, 0, 0, 0)),
        out_shape=jax.ShapeDtypeStruct((bsz, n_heads, nt, head_dim, tile), BF16),
        scratch_shapes=([pltpu.VMEM((tile, tile), F32)] * ATTN_DEPTH + [pltpu.VMEM((tile, tile), BF16)] * ATTN_DEPTH
                        + [pltpu.VMEM((1, tile), F32)] * (2 * ATTN_DEPTH)
                        + [pltpu.VMEM((1, tile), F32), pltpu.VMEM((LANES, tile), F32)]),
        compiler_params=_params(("arbitrary", "arbitrary")),
    )(qt, kx, vt)


def _oproj_kernel(o_ref, wo_ref, h_ref, g_ref, b_ref, rw_ref, rb_ref,
                  hn_ref, hnt_ref, ti_ref, gate_ref, rank_ref, cnt_ref, run_ref, *, alpha):
    bi = pl.program_id(0)
    i = pl.program_id(1)

    @pl.when((bi == 0) & (i == 0))
    def _():
        run_ref[...] = jnp.zeros_like(run_ref)

    n_heads, _, head_dim, tile = o_ref.shape[1:]
    o_t = o_ref[0].reshape(n_heads * head_dim, tile)
    att = lax.dot_general(o_t, wo_ref[...], (((0,), (0,)), ((), ())), preferred_element_type=F32)
    hn = _layer_norm(alpha * h_ref[...] + att, g_ref[...], b_ref[...])
    hn_ref[...] = hn
    _store_row_tiles(hnt_ref, hn)
    _route(hn, rw_ref, rb_ref, run_ref, ti_ref, gate_ref, rank_ref, cnt_ref)


def _oproj_call(o, w_out, h, ln_g, ln_b, router_w, router_b, *, bsz, lp, alpha):
    tp, d = h.shape
    tile = SEQ_TILE
    nt = lp // tile
    n_exp = router_w.shape[1]
    row = lambda b, i: (b * nt + i, 0)
    return pl.pallas_call(
        functools.partial(_oproj_kernel, alpha=alpha),
        grid=(bsz, nt),
        in_specs=[pl.BlockSpec((1,) + o.shape[1:2] + (1,) + o.shape[3:], lambda b, i: (b, 0, i, 0, 0)),
                  _const_spec((d, d)),
                  pl.BlockSpec((tile, d), row),
                  _const_spec((1, d)), _const_spec((1, d)),
                  _const_spec((d, n_exp)), _const_spec((1, n_exp))],
        out_specs=[pl.BlockSpec((tile, d), row), _row_tile_spec(nt, tile, d)] + _route_out_specs(nt, tile, n_exp),
        out_shape=[jax.ShapeDtypeStruct((tp, d), F32), jax.ShapeDtypeStruct((tp, d // LANES, LANES), F32)]
        + _route_out_shapes(tp, n_exp),
        scratch_shapes=[pltpu.VMEM((1, n_exp), F32)],
        compiler_params=_params(("arbitrary", "arbitrary")),
    )(o, w_out.astype(BF16), h, ln_g.reshape(1, d), ln_b.reshape(1, d),
      router_w, router_b.reshape(1, n_exp))


def kernel(x, meta_tokens, pool_w, pool_scale, attn_w_in, attn_b_f, attn_w_out,
           ln_g, ln_b, router_w, router_b, w1, b1, w2, b2):
    bsz, seq, d = x.shape
    n_meta = meta_tokens.shape[0]
    depth = ln_g.shape[0]
    n_heads = attn_b_f.shape[-1]
    head_dim = d // n_heads
    alpha = float((2 * depth) ** 0.25)
    length = n_meta + seq
    lp = -(-length // SEQ_TILE) * SEQ_TILE
    assert d % (len(POOL_WINDOWS) * LANES) == 0 and head_dim % 16 == 0 and head_dim + 16 <= LANES and depth == 2

    meta = jnp.broadcast_to(meta_tokens[None], (bsz, n_meta, d))
    h = jnp.concatenate([meta, x, jnp.zeros((bsz, lp - length, d), x.dtype)], axis=1)
    h = h.reshape(bsz * lp, d)

    h, h_tiles, top_i, gates, rank, counts = _pool_call(
        h, pool_w[0], pool_scale[0], ln_g[0, 0], ln_b[0, 0], router_w[0], router_b[0],
        bsz=bsz, lp=lp, alpha=alpha)
    h = _moe(h, h_tiles, top_i, gates, rank, counts, 0, w1, b1, w2, b2, ln_g[0, 1], ln_b[0, 1], alpha=alpha)

    weights = _attn_weights(attn_w_in[0], attn_b_f[0], n_heads, head_dim)
    qt, kx, vt = _proj_call(h, weights, bsz=bsz, lp=lp, n_heads=n_heads, head_dim=head_dim)
    o = _attn_call(qt, kx, vt, head_dim=head_dim)
    h, h_tiles, top_i, gates, rank, counts = _oproj_call(
        o, attn_w_out[0], h, ln_g[1, 0], ln_b[1, 0], router_w[1], router_b[1],
        bsz=bsz, lp=lp, alpha=alpha)
    h = _moe(h, h_tiles, top_i, gates, rank, counts, 1, w1, b1, w2, b2, ln_g[1, 1], ln_b[1, 1], alpha=alpha)

    return h.reshape(bsz, lp, d)[:, n_meta:length]
```

```python
import functools

import numpy as np
import jax
import jax.numpy as jnp
from jax import lax
from jax.experimental import pallas as pl
from jax.experimental.pallas import tpu as pltpu

POOL_WINDOWS = (2, 4, 8, 16)
MAX_WIN = max(POOL_WINDOWS)
TOP_K = 4
SWIGLU_LIMIT = 7.0
SWIGLU_ALPHA = 1.702
LN_EPS = 1e-5
MASK_VALUE = -1e30

LANES = 128
SEQ_TILE = 512
ROUTE_TILE = 512
EXPERT_ROWS = 512
VMEM_LIMIT = 56 * 1024 * 1024
LOG2E = 1.4426950408889634
ATTN_DEPTH = 3

F32 = jnp.float32
BF16 = jnp.bfloat16


def _params(sem, vmem=VMEM_LIMIT):
    return pltpu.CompilerParams(dimension_semantics=sem, vmem_limit_bytes=vmem)


def _dot3(a, b):
    a_hi = a.astype(BF16)
    a_lo = (a - a_hi.astype(F32)).astype(BF16)
    b_hi = b.astype(BF16)
    b_lo = (b - b_hi.astype(F32)).astype(BF16)
    dot = functools.partial(jnp.dot, preferred_element_type=F32)
    return dot(a_hi, b_hi) + dot(a_hi, b_lo) + dot(a_lo, b_hi)


def _layer_norm(z, g, b):
    mu = jnp.mean(z, axis=-1, keepdims=True)
    d = z - mu
    var = jnp.mean(d * d, axis=-1, keepdims=True)
    return d * lax.rsqrt(var + LN_EPS) * g + b


def _route(hn, rw_ref, rb_ref, run_ref, ti_ref, gate_ref, rank_ref, cnt_ref):
    rows = hn.shape[0]
    logits = _dot3(hn, rw_ref[...]) + rb_ref[...]
    n_exp = logits.shape[1]
    lane = lax.broadcasted_iota(jnp.int32, (rows, n_exp), 1)
    cur = logits
    vals, idxs, hots = [], [], []
    for _ in range(TOP_K):
        m = jnp.max(cur, axis=-1, keepdims=True)
        idx = jnp.min(jnp.where(cur == m, lane, n_exp), axis=-1, keepdims=True)
        hot = lane == idx
        vals.append(m)
        idxs.append(idx)
        hots.append(hot)
        cur = jnp.where(hot, -jnp.inf, cur)
    exps = [jnp.exp(v - vals[0]) for v in vals]
    denom = exps[0]
    for e in exps[1:]:
        denom = denom + e
    gates = [e / denom for e in exps]
    sel = hots[0].astype(F32)
    for hot in hots[1:]:
        sel = sel + hot.astype(F32)
    r = lax.broadcasted_iota(jnp.int32, (rows, rows), 0)
    c = lax.broadcasted_iota(jnp.int32, (rows, rows), 1)
    tri = (c < r).astype(BF16)
    base = run_ref[...] + jnp.dot(tri, sel.astype(BF16), preferred_element_type=F32)
    ranks = [jnp.sum(jnp.where(hot, base, 0.0), axis=-1, keepdims=True) for hot in hots]
    run_ref[...] = run_ref[...] + jnp.sum(sel, axis=0, keepdims=True)
    cnt_ref[...] = run_ref[...]
    lane_k = lax.broadcasted_iota(jnp.int32, (rows, TOP_K), 1)

    def pack(cols):
        out = jnp.broadcast_to(cols[TOP_K - 1], (rows, TOP_K))
        for k in range(TOP_K - 2, -1, -1):
            out = jnp.where(lane_k == k, cols[k], out)
        return out

    ti_ref[...] = pack(idxs)
    gate_ref[...] = pack(gates)
    rank_ref[...] = pack(ranks).astype(jnp.int32)


def _pool_kernel(h_ref, pw_ref, ps_ref, g_ref, b_ref, rw_ref, rb_ref,
                 h1_ref, h1t_ref, ti_ref, gate_ref, rank_ref, cnt_ref,
                 ext_ref, run_ref, *, tile, alpha):
    bi = pl.program_id(0)
    i = pl.program_id(1)
    d_model = h_ref.shape[1]
    gdim = d_model // len(POOL_WINDOWS)

    @pl.when(i == 0)
    def _():
        ext_ref[0:MAX_WIN, :] = jnp.zeros((MAX_WIN, d_model), F32)

    @pl.when((bi == 0) & (i == 0))
    def _():
        run_ref[...] = jnp.zeros_like(run_ref)

    x = h_ref[...]
    ext_ref[MAX_WIN:MAX_WIN + tile, :] = x
    pos = i * tile + lax.broadcasted_iota(jnp.int32, (tile, 1), 0)
    ys = []
    for g, w in enumerate(POOL_WINDOWS):
        lo, hi = g * gdim, (g + 1) * gdim
        xg = x[:, lo:hi]
        s = xg
        for j in range(1, w):
            s = s + ext_ref[MAX_WIN - j:MAX_WIN - j + tile, lo:hi]
        cnt = jnp.minimum(pos + 1, w).astype(F32)
        u = s / cnt - xg
        ys.append(jnp.dot(u.astype(BF16), pw_ref[g].astype(BF16), preferred_element_type=F32))
    y = jnp.concatenate(ys, axis=-1) * ps_ref[...]
    hn = _layer_norm(alpha * x + y, g_ref[...], b_ref[...])
    h1_ref[...] = hn
    _store_row_tiles(h1t_ref, hn)
    ext_ref[0:MAX_WIN, :] = x[tile - MAX_WIN:tile, :]
    _route(hn, rw_ref, rb_ref, run_ref, ti_ref, gate_ref, rank_ref, cnt_ref)


def _store_row_tiles(out_ref, x):
    for c in range(out_ref.shape[1]):
        out_ref[:, c, :] = x[:, c * LANES:(c + 1) * LANES]


def _load_row_tiles(ref):
    return jnp.concatenate([ref[:, c, :] for c in range(ref.shape[1])], axis=-1)


def _route_out_shapes(tp, n_exp):
    return [jax.ShapeDtypeStruct((tp, TOP_K), jnp.int32),
            jax.ShapeDtypeStruct((tp, TOP_K), F32),
            jax.ShapeDtypeStruct((tp, TOP_K), jnp.int32),
            jax.ShapeDtypeStruct((1, n_exp), F32)]


def _route_out_specs(nt, tile, n_exp):
    row = lambda b, i: (b * nt + i, 0)
    return [pl.BlockSpec((tile, TOP_K), row),
            pl.BlockSpec((tile, TOP_K), row),
            pl.BlockSpec((tile, TOP_K), row),
            pl.BlockSpec((1, n_exp), lambda b, i: (0, 0))]


def _row_tile_spec(nt, tile, d):
    return pl.BlockSpec((tile, d // LANES, LANES), lambda b, i: (b * nt + i, 0, 0))


def _const_spec(shape):
    return pl.BlockSpec(shape, lambda b, i: (0,) * len(shape))


def _pool_call(h, pool_w, pool_scale, ln_g, ln_b, router_w, router_b, *, bsz, lp, alpha):
    tp, d = h.shape
    tile = SEQ_TILE
    nt = lp // tile
    n_exp = router_w.shape[1]
    groups, gdim, _ = pool_w.shape
    row = lambda b, i: (b * nt + i, 0)
    return pl.pallas_call(
        functools.partial(_pool_kernel, tile=tile, alpha=alpha),
        grid=(bsz, nt),
        in_specs=[pl.BlockSpec((tile, d), row),
                  _const_spec((groups, gdim, gdim)),
                  _const_spec((1, d)), _const_spec((1, d)), _const_spec((1, d)),
                  _const_spec((d, n_exp)), _const_spec((1, n_exp))],
        out_specs=[pl.BlockSpec((tile, d), row), _row_tile_spec(nt, tile, d)] + _route_out_specs(nt, tile, n_exp),
        out_shape=[jax.ShapeDtypeStruct((tp, d), F32), jax.ShapeDtypeStruct((tp, d // LANES, LANES), F32)]
        + _route_out_shapes(tp, n_exp),
        scratch_shapes=[pltpu.VMEM((MAX_WIN + tile, d), F32), pltpu.VMEM((1, n_exp), F32)],
        compiler_params=_params(("arbitrary", "arbitrary")),
    )(h, pool_w, pool_scale.reshape(1, d), ln_g.reshape(1, d), ln_b.reshape(1, d),
      router_w, router_b.reshape(1, n_exp))


def _plan(counts, top_i, rank, n_blocks):
    tp = top_i.shape[0]
    n_exp = counts.shape[1]
    n_rows = n_blocks * EXPERT_ROWS
    cnt = counts.reshape(n_exp).astype(jnp.int32)
    padded = (cnt + EXPERT_ROWS - 1) // EXPERT_ROWS * EXPERT_ROWS
    pad_ends = jnp.cumsum(padded)
    pad_starts = pad_ends - padded
    hot = top_i[..., None] == jnp.arange(n_exp, dtype=jnp.int32)
    dest = (jnp.sum(jnp.where(hot, pad_starts, 0), axis=-1) + rank).reshape(-1)
    starts = jnp.arange(n_blocks, dtype=jnp.int32) * EXPERT_ROWS
    block_e = jnp.sum((starts[:, None] >= pad_ends[None, :]).astype(jnp.int32), axis=1)
    block_e = jnp.minimum(block_e, n_exp - 1)
    n_used = (pad_ends[-1] // EXPERT_ROWS).reshape(1)
    tails = jnp.minimum(pad_starts + cnt, n_rows - EXPERT_ROWS)

    copy = jnp.arange(tp * TOP_K, dtype=jnp.int32)
    row_copy = jnp.full((n_rows,), -1, jnp.int32).at[dest].set(copy, unique_indices=True)
    valid = row_copy >= 0
    tok, k = row_copy // TOP_K, row_copy % TOP_K
    scrap = tp * TOP_K + jnp.cumsum(jnp.logical_not(valid).astype(jnp.int32)) - 1
    row_dst = jnp.where(valid, k * tp + tok, scrap)
    first = n_rows + jnp.arange(EXPERT_ROWS, dtype=jnp.int32) % (EXPERT_ROWS // 2)
    dst_ext = jnp.concatenate([first, row_dst])
    return (block_e, n_used, tails, dest.reshape(-1, 1, SEQ_TILE * TOP_K),
            dst_ext.reshape(n_blocks + 1, 1, EXPERT_ROWS))


def _dispatch_kernel(nu_ref, tails_ref, dest_ref, h_ref, xs_ref, zero_ref, sem, *, n_blocks):
    i = pl.program_id(0)
    tile = h_ref.shape[0]
    rb = zero_ref.shape[0]

    @pl.when(i == 0)
    def _():
        zero_ref[...] = jnp.zeros_like(zero_ref)
        fills = [pltpu.make_async_copy(zero_ref, xs_ref.at[pl.ds(tails_ref[e], rb)], sem)
                 for e in range(tails_ref.shape[0])]
        for cp in fills:
            cp.start()
        for cp in fills:
            cp.wait()

        def fill_block(blk, carry):
            cp = pltpu.make_async_copy(zero_ref, xs_ref.at[pl.ds(blk * rb, rb)], sem)
            cp.start()
            cp.wait()
            return carry

        lax.fori_loop(nu_ref[0], n_blocks, fill_block, 0)

    for r in range(tile):
        for k in range(TOP_K):
            pltpu.make_async_copy(h_ref.at[r], xs_ref.at[dest_ref[0, 0, r * TOP_K + k]], sem).start()
    for _ in range(TOP_K):
        pltpu.make_async_copy(h_ref, xs_ref.at[pl.ds(0, tile)], sem).wait()


def _dispatch_call(n_used, tails, dest, h_tiles, n_blocks):
    tp, chunks, _ = h_tiles.shape
    tile = SEQ_TILE
    rb = EXPERT_ROWS
    grid_spec = pltpu.PrefetchScalarGridSpec(
        num_scalar_prefetch=2,
        grid=(tp // tile,),
        in_specs=[pl.BlockSpec((1, 1, tile * TOP_K), lambda i, nu, tl: (i, 0, 0), memory_space=pltpu.SMEM),
                  pl.BlockSpec((tile, chunks, LANES), lambda i, nu, tl: (i, 0, 0))],
        out_specs=pl.BlockSpec(memory_space=pl.ANY),
        scratch_shapes=[pltpu.VMEM((rb, chunks, LANES), F32), pltpu.SemaphoreType.DMA(())],
    )
    return pl.pallas_call(
        functools.partial(_dispatch_kernel, n_blocks=n_blocks),
        grid_spec=grid_spec,
        out_shape=jax.ShapeDtypeStruct((n_blocks * rb, chunks, LANES), F32),
        compiler_params=_params(("arbitrary",)),
    )(n_used, tails, dest, h_tiles)


def _ffn_kernel(be_ref, nu_ref, dstp_ref, dst_ref, x_ref, w1_ref, b1_ref, w2_ref, b2_ref,
                y_ref, ya_ref, yb_ref, w1b_ref, w2b_ref, ssem):
    i = pl.program_id(0)
    half = ya_ref.shape[0]
    f = w2b_ref.shape[0]

    def scatter(idx_ref, base, buf_ref, sem):
        for r in range(half):
            pltpu.make_async_copy(buf_ref.at[r], y_ref.at[idx_ref[0, 0, base + r]], sem).start()

    def wait_scatter(buf_ref, sem):
        pltpu.make_async_copy(buf_ref, y_ref.at[pl.ds(0, half)], sem).wait()

    def ffn(rows_ref, out_ref):
        hid = jnp.dot(_load_row_tiles(rows_ref).astype(BF16), w1b_ref[...], preferred_element_type=F32) + b1_ref[0]
        gate = jnp.minimum(hid[:, :f], SWIGLU_LIMIT)
        up = jnp.clip(hid[:, f:], -SWIGLU_LIMIT, SWIGLU_LIMIT)
        act = gate * jax.nn.sigmoid(SWIGLU_ALPHA * gate) * (up + 1.0)
        _store_row_tiles(out_ref, jnp.dot(act.astype(BF16), w2b_ref[...], preferred_element_type=F32) + b2_ref[0])

    @pl.when(i < nu_ref[0])
    def _():
        @pl.when(i == 0)
        def _():
            yb_ref[...] = jnp.zeros_like(yb_ref)

        @pl.when((i == 0) | (be_ref[i] != be_ref[jnp.maximum(i - 1, 0)]))
        def _():
            w1b_ref[...] = w1_ref[0, 0].astype(BF16)
            w2b_ref[...] = w2_ref[0, 0].astype(BF16)

        @pl.when(i >= 1)
        def _():
            wait_scatter(ya_ref, ssem.at[0])

        scatter(dstp_ref, half, yb_ref, ssem.at[1])
        ffn(x_ref.at[pl.ds(0, half)], ya_ref)

        wait_scatter(yb_ref, ssem.at[1])
        scatter(dst_ref, 0, ya_ref, ssem.at[0])
        ffn(x_ref.at[pl.ds(half, half)], yb_ref)

        @pl.when(i == nu_ref[0] - 1)
        def _():
            scatter(dst_ref, half, yb_ref, ssem.at[1])
            wait_scatter(ya_ref, ssem.at[0])
            wait_scatter(yb_ref, ssem.at[1])

    @pl.when(i >= nu_ref[0])
    def _():
        ya_ref[...] = jnp.zeros_like(ya_ref)
        first = dst_ref[0, 0, 0]
        copies = [pltpu.make_async_copy(ya_ref, y_ref.at[pl.ds(first + j * half, half)], ssem.at[j])
                  for j in range(2)]
        for cp in copies:
            cp.start()
        for cp in copies:
            cp.wait()


def _ffn_call(block_e, n_used, dst_ext, xs, layer, w1, b1, w2, b2):
    n_rows, chunks, _ = xs.shape
    d = chunks * LANES
    depth, n_exp, _, f2 = w1.shape
    f = w2.shape[2]
    rb = EXPERT_ROWS
    nb = n_rows // rb
    half = rb // 2
    n_out = n_rows + half

    def blk(i, be, nu):
        return jnp.minimum(i, nu[0] - 1)

    def idx_spec(index_map):
        return pl.BlockSpec((1, 1, rb), index_map, memory_space=pltpu.SMEM)

    grid_spec = pltpu.PrefetchScalarGridSpec(
        num_scalar_prefetch=2,
        grid=(nb,),
        in_specs=[idx_spec(lambda i, be, nu: (i, 0, 0)),
                  idx_spec(lambda i, be, nu: (i + 1, 0, 0)),
                  pl.BlockSpec((rb, chunks, LANES), lambda i, be, nu: (blk(i, be, nu), 0, 0)),
                  pl.BlockSpec((1, 1, d, f2), lambda i, be, nu: (layer, be[blk(i, be, nu)], 0, 0)),
                  pl.BlockSpec((1, 1, f2), lambda i, be, nu: (layer * n_exp + be[blk(i, be, nu)], 0, 0)),
                  pl.BlockSpec((1, 1, f, d), lambda i, be, nu: (layer, be[blk(i, be, nu)], 0, 0)),
                  pl.BlockSpec((1, 1, d), lambda i, be, nu: (layer * n_exp + be[blk(i, be, nu)], 0, 0))],
        out_specs=pl.BlockSpec(memory_space=pl.ANY),
        scratch_shapes=[pltpu.VMEM((half, chunks, LANES), F32)] * 2 + [
            pltpu.VMEM((d, f2), BF16), pltpu.VMEM((f, d), BF16), pltpu.SemaphoreType.DMA((2,))],
    )
    return pl.pallas_call(
        _ffn_kernel,
        grid_spec=grid_spec,
        out_shape=jax.ShapeDtypeStruct((n_out, chunks, LANES), F32),
        compiler_params=_params(("arbitrary",)),
    )(block_e, n_used, dst_ext, dst_ext, xs,
      w1, b1.reshape(depth * n_exp, 1, f2), w2, b2.reshape(depth * n_exp, 1, d))


def _combine_kernel(gate_ref, h_ref, g_ref, b_ref, *refs, alpha):
    y_refs, out_ref = refs[:TOP_K], refs[TOP_K]
    gates = gate_ref[...]
    ffn = gates[:, 0:1] * _load_row_tiles(y_refs[0])
    for k in range(1, TOP_K):
        ffn = ffn + gates[:, k:k + 1] * _load_row_tiles(y_refs[k])
    out_ref[...] = _layer_norm(alpha * h_ref[...] + ffn, g_ref[...], b_ref[...])


def _combine_call(gates, h, y, ln_g, ln_b, *, alpha):
    tp, d = h.shape
    tile = ROUTE_TILE
    nt = tp // tile
    y_specs = [pl.BlockSpec((tile,) + y.shape[1:], functools.partial(lambda k, i: (k * nt + i, 0, 0), k))
               for k in range(TOP_K)]
    return pl.pallas_call(
        functools.partial(_combine_kernel, alpha=alpha),
        grid=(nt,),
        in_specs=[pl.BlockSpec((tile, TOP_K), lambda i: (i, 0)),
                  pl.BlockSpec((tile, d), lambda i: (i, 0)),
                  pl.BlockSpec((1, d), lambda i: (0, 0)),
                  pl.BlockSpec((1, d), lambda i: (0, 0))] + y_specs,
        out_specs=pl.BlockSpec((tile, d), lambda i: (i, 0)),
        out_shape=jax.ShapeDtypeStruct((tp, d), F32),
        compiler_params=_params(("arbitrary",)),
    )(gates, h, ln_g.reshape(1, d), ln_b.reshape(1, d), *([y] * TOP_K))


def _moe(h, h_tiles, top_i, gates, rank, counts, layer, w1, b1, w2, b2, ln_g, ln_b, *, alpha):
    tp = h.shape[0]
    n_exp = w1.shape[1]
    n_blocks = tp * TOP_K // EXPERT_ROWS + n_exp
    block_e, n_used, tails, dest, dst_ext = _plan(counts, top_i, rank, n_blocks)
    xs = _dispatch_call(n_used, tails, dest, h_tiles, n_blocks)
    y = _ffn_call(block_e, n_used, dst_ext, xs, layer, w1, b1, w2, b2)
    return _combine_call(gates, h, y, ln_g, ln_b, alpha=alpha)


def _log_sigmoid(x):
    return jnp.minimum(x, 0.0) - jnp.log(1.0 + jnp.exp(-jnp.abs(x)))


def _split3(c):
    hi = c.astype(BF16)
    r1 = c - hi.astype(F32)
    mid = r1.astype(BF16)
    lo = (r1 - mid.astype(F32)).astype(BF16)
    return hi, mid, lo


def _proj_kernel(h_ref, wqt_ref, wk_ref, wvt_ref, wf_ref, bf_ref, selk_ref, onek_ref,
                 qt_ref, kx_ref, vt_ref, carry_ref, *, tile, n_heads, head_dim):
    i = pl.program_id(1)

    @pl.when(i == 0)
    def _():
        carry_ref[...] = jnp.zeros_like(carry_ref)

    x = h_ref[...]
    xb = x.astype(BF16)
    nt_dims = (((1,), (1,)), ((), ()))
    lf = _log_sigmoid(_dot3(x, wf_ref[...]) + bf_ref[...])
    r = lax.broadcasted_iota(jnp.int32, (tile, tile), 0)
    c = lax.broadcasted_iota(jnp.int32, (tile, tile), 1)
    tril = (c <= r).astype(BF16)
    cs = carry_ref[...]
    for part in _split3(lf):
        cs = cs + jnp.dot(tril, part, preferred_element_type=F32)
    carry_ref[...] = cs[tile - 1:tile, :]
    parts = _split3(cs * LOG2E)
    eye = (r == c).astype(BF16)
    parts_t = [lax.dot_general(p, eye, (((0,), (0,)), ((), ())), preferred_element_type=F32)
               for p in parts]

    kx = jnp.dot(xb, wk_ref[...], preferred_element_type=F32) + onek_ref[...]
    for p, part in enumerate(parts):
        kx = kx + jnp.dot(-part, selk_ref[p], preferred_element_type=F32)
    qt = lax.dot_general(wqt_ref[...], xb, nt_dims, preferred_element_type=F32)
    vt = lax.dot_general(wvt_ref[...], xb, nt_dims, preferred_element_type=F32)
    aug_rows = 16
    row = lax.broadcasted_iota(jnp.int32, (aug_rows, tile), 0)
    v_aug = jnp.where(row == 0, 1.0, 0.0).astype(BF16)
    rest = jnp.zeros((LANES - head_dim - aug_rows, tile), BF16)
    for hd in range(n_heads):
        lo = hd * head_dim
        q_aug = jnp.where(row < 6, 1.0, 0.0)
        for p in range(2, -1, -1):
            q_aug = jnp.where(row == p, parts_t[p][hd:hd + 1, :], q_aug)
        qt_ref[0, hd, 0, 0:head_dim, :] = qt[lo:lo + head_dim, :].astype(BF16)
        qt_ref[0, hd, 0, head_dim:head_dim + aug_rows, :] = q_aug.astype(BF16)
        qt_ref[0, hd, 0, head_dim + aug_rows:, :] = rest
        vt_ref[0, hd, 0, 0:head_dim, :] = vt[lo:lo + head_dim, :].astype(BF16)
        vt_ref[0, hd, 0, head_dim:head_dim + aug_rows, :] = v_aug
        vt_ref[0, hd, 0, head_dim + aug_rows:, :] = rest
        kx_ref[0, hd, 0] = kx[:, hd * LANES:(hd + 1) * LANES].astype(BF16)


def _attn_weights(w_in, b_f, n_heads, head_dim):
    d = w_in.shape[0]
    scale = head_dim ** -0.5 * LOG2E
    hw = n_heads * LANES
    wk = w_in[:, d:2 * d].reshape(d, n_heads, head_dim)
    wk = jnp.pad(wk, ((0, 0), (0, 0), (0, LANES - head_dim))).reshape(d, hw).astype(BF16)
    wqt = (w_in[:, :d] * scale).T.astype(BF16)
    wvt = w_in[:, 2 * d:3 * d].T.astype(BF16)
    selk = np.zeros((3, n_heads, hw), np.float32)
    onek = np.zeros((1, hw), np.float32)
    for h in range(n_heads):
        base = h * LANES + head_dim
        for p in range(3):
            selk[p, h, base + 3 + p] = 1.0
            onek[0, base + p] = 1.0
    return (wqt, wk, wvt, w_in[:, 3 * d:], b_f.reshape(1, n_heads), jnp.asarray(selk, BF16), jnp.asarray(onek))


def _proj_call(h, weights, *, bsz, lp, n_heads, head_dim):
    tp, d = h.shape
    tile = SEQ_TILE
    nt = lp // tile
    row = lambda b, i: (b * nt + i, 0)
    in_specs = [pl.BlockSpec((tile, d), row)] + [_const_spec(w.shape) for w in weights]
    t_spec = pl.BlockSpec((1, n_heads, 1, LANES, tile), lambda b, i: (b, 0, i, 0, 0))
    k_spec = pl.BlockSpec((1, n_heads, 1, tile, LANES), lambda b, i: (b, 0, i, 0, 0))
    return pl.pallas_call(
        functools.partial(_proj_kernel, tile=tile, n_heads=n_heads, head_dim=head_dim),
        grid=(bsz, nt),
        in_specs=in_specs,
        out_specs=[t_spec, k_spec, t_spec],
        out_shape=[jax.ShapeDtypeStruct((bsz, n_heads, nt, LANES, tile), BF16),
                   jax.ShapeDtypeStruct((bsz, n_heads, nt, tile, LANES), BF16),
                   jax.ShapeDtypeStruct((bsz, n_heads, nt, LANES, tile), BF16)],
        scratch_shapes=[pltpu.VMEM((1, n_heads), F32)],
        compiler_params=_params(("arbitrary", "arbitrary")),
    )(h, *weights)


def _attn_kernel(qt_ref, kx_ref, vt_ref, o_ref, s0_ref, s1_ref, s2_ref, p0_ref, p1_ref, p2_ref,
                 a0_ref, a1_ref, a2_ref, c0_ref, c1_ref, c2_ref, m_ref, acc_ref, *, nt, tile, head_dim):
    s_refs, p_refs, a_refs = (s0_ref, s1_ref, s2_ref), (p0_ref, p1_ref, p2_ref), (a0_ref, a1_ref, a2_ref)
    c_refs = (c0_ref, c1_ref, c2_ref)
    key = lax.broadcasted_iota(jnp.int32, (tile, tile), 0)
    qry = lax.broadcasted_iota(jnp.int32, (tile, tile), 1)

    def logits(slot, qi, kj):
        s = jnp.dot(kx_ref[0, 0, kj], qt_ref[0, 0, qi], preferred_element_type=F32)
        s_refs[slot][...] = s
        c_refs[slot][...] = jnp.max(s, axis=0, keepdims=True)

    def value_update(slot, kj, out_qi=None):
        acc = a_refs[slot][...] * acc_ref[...] + jnp.dot(
            vt_ref[0, 0, kj], p_refs[slot][...], preferred_element_type=F32)
        acc_ref[...] = acc
        if out_qi is not None:
            o_ref[0, 0, out_qi] = (acc[:head_dim, :] * (1.0 / acc[head_dim:head_dim + 1, :])).astype(BF16)

    def substep(slot, qi, kj, *, diag, write_out):
        if diag:
            nqi, nkj = qi + 1, jnp.int32(1)
        else:
            stay = qi - kj >= 2
            nqi, nkj = jnp.where(stay, qi, qi + 1), jnp.where(stay, kj + 2, 0)
        logits((slot + 2) % ATTN_DEPTH, jnp.minimum(nqi, nt - 1), nkj)
        pkj = jnp.where(kj >= 2, kj - 2, jnp.maximum(qi - 1 - jnp.where(kj == 0, 1, 0), 0))
        out_qi = jnp.where(kj == 1, qi - 1, qi) if write_out else None
        value_update((slot + 1) % ATTN_DEPTH, pkj, out_qi)
        m_old = m_ref[...]
        if diag:
            s = jnp.where(key <= qry, s_refs[slot][...], MASK_VALUE)
            m_new = jnp.maximum(m_old, jnp.max(s, axis=0, keepdims=True))
            p_refs[slot][...] = jnp.exp2(s - m_new).astype(BF16)
            m_ref[...] = jnp.full_like(m_old, MASK_VALUE)
        else:
            m_new = jnp.maximum(m_old, c_refs[slot][...])
            p_refs[slot][...] = jnp.exp2(s_refs[slot][...] - m_new).astype(BF16)
            m_ref[...] = m_new
        a_refs[slot][...] = jnp.exp2(m_old - m_new)

    def q_tile(slot, qi, n_loops, rem):
        def trip(i, carry):
            for j in range(ATTN_DEPTH):
                substep((slot + j) % ATTN_DEPTH, qi, ATTN_DEPTH * i + j, diag=False, write_out=j == 1)
            return carry

        lax.fori_loop(0, n_loops, trip, 0)
        for j in range(rem):
            substep((slot + j) % ATTN_DEPTH, qi, ATTN_DEPTH * n_loops + j, diag=False, write_out=j == 1)
        substep((slot + rem) % ATTN_DEPTH, qi, qi, diag=True, write_out=rem == 1)
        return (slot + rem + 1) % ATTN_DEPTH

    m_ref[...] = jnp.full_like(m_ref, MASK_VALUE)
    acc_ref[...] = jnp.ones_like(acc_ref)
    for slot in range(1, ATTN_DEPTH):
        p_refs[slot][...] = jnp.zeros_like(p_refs[slot])
        a_refs[slot][...] = jnp.ones_like(a_refs[slot])
    logits(0, 0, 0)
    logits(1, 1, 0)

    def group(g, carry):
        slot = 0
        for r in range(ATTN_DEPTH):
            slot = q_tile(slot, ATTN_DEPTH * g + r, g, r)
        assert slot == 0
        return carry

    lax.fori_loop(0, nt // ATTN_DEPTH, group, 0)
    slot = 0
    for qi in range(nt // ATTN_DEPTH * ATTN_DEPTH, nt):
        slot = q_tile(slot, jnp.int32(qi), jnp.int32(qi // ATTN_DEPTH), qi % ATTN_DEPTH)
    value_update((slot + 1) % ATTN_DEPTH, nt - 2)
    value_update((slot + 2) % ATTN_DEPTH, nt - 1, nt - 1)


def _attn_call(qt, kx, vt, *, head_dim):
    bsz, n_heads, nt, _, tile = qt.shape
    return pl.pallas_call(
        functools.partial(_attn_kernel, nt=nt, tile=tile, head_dim=head_dim),
        grid=(bsz, n_heads),
        in_specs=[pl.BlockSpec((1, 1, nt, LANES, tile), lambda b, h: (b, h, 0, 0, 0)),
                  pl.BlockSpec((1, 1, nt, tile, LANES), lambda b, h: (b, h, 0, 0, 0)),
                  pl.BlockSpec((1, 1, nt, LANES, tile), lambda b, h: (b, h, 0, 0, 0))],
        out_specs=pl.BlockSpec((1, 1, nt, head_dim, tile), lambda b, h: (b, h, 0, 0, 0)),
        out_shape=jax.ShapeDtypeStruct((bsz, n_heads, nt, head_dim, tile), BF16),
        scratch_shapes=([pltpu.VMEM((tile, tile), F32)] * ATTN_DEPTH + [pltpu.VMEM((tile, tile), BF16)] * ATTN_DEPTH
                        + [pltpu.VMEM((1, tile), F32)] * (2 * ATTN_DEPTH)
                        + [pltpu.VMEM((1, tile), F32), pltpu.VMEM((LANES, tile), F32)]),
        compiler_params=_params(("arbitrary", "arbitrary")),
    )(qt, kx, vt)


def _oproj_kernel(o_ref, wo_ref, h_ref, g_ref, b_ref, rw_ref, rb_ref,
                  hn_ref, hnt_ref, ti_ref, gate_ref, rank_ref, cnt_ref, run_ref, *, alpha):
    bi = pl.program_id(0)
    i = pl.program_id(1)

    @pl.when((bi == 0) & (i == 0))
    def _():
        run_ref[...] = jnp.zeros_like(run_ref)

    n_heads, _, head_dim, tile = o_ref.shape[1:]
    o_t = o_ref[0].reshape(n_heads * head_dim, tile)
    att = lax.dot_general(o_t, wo_ref[...], (((0,), (0,)), ((), ())), preferred_element_type=F32)
    hn = _layer_norm(alpha * h_ref[...] + att, g_ref[...], b_ref[...])
    hn_ref[...] = hn
    _store_row_tiles(hnt_ref, hn)
    _route(hn, rw_ref, rb_ref, run_ref, ti_ref, gate_ref, rank_ref, cnt_ref)


def _oproj_call(o, w_out, h, ln_g, ln_b, router_w, router_b, *, bsz, lp, alpha):
    tp, d = h.shape
    tile = SEQ_TILE
    nt = lp // tile
    n_exp = router_w.shape[1]
    row = lambda b, i: (b * nt + i, 0)
    return pl.pallas_call(
        functools.partial(_oproj_kernel, alpha=alpha),
        grid=(bsz, nt),
        in_specs=[pl.BlockSpec((1,) + o.shape[1:2] + (1,) + o.shape[3:], lambda b, i: (b, 0, i, 0, 0)),
                  _const_spec((d, d)),
                  pl.BlockSpec((tile, d), row),
                  _const_spec((1, d)), _const_spec((1, d)),
                  _const_spec((d, n_exp)), _const_spec((1, n_exp))],
        out_specs=[pl.BlockSpec((tile, d), row), _row_tile_spec(nt, tile, d)] + _route_out_specs(nt, tile, n_exp),
        out_shape=[jax.ShapeDtypeStruct((tp, d), F32), jax.ShapeDtypeStruct((tp, d // LANES, LANES), F32)]
        + _route_out_shapes(tp, n_exp),
        scratch_shapes=[pltpu.VMEM((1, n_exp), F32)],
        compiler_params=_params(("arbitrary", "arbitrary")),
    )(o, w_out.astype(BF16), h, ln_g.reshape(1, d), ln_b.reshape(1, d),
      router_w, router_b.reshape(1, n_exp))


def kernel(x, meta_tokens, pool_w, pool_scale, attn_w_in, attn_b_f, attn_w_out,
           ln_g, ln_b, router_w, router_b, w1, b1, w2, b2):
    bsz, seq, d = x.shape
    n_meta = meta_tokens.shape[0]
    depth = ln_g.shape[0]
    n_heads = attn_b_f.shape[-1]
    head_dim = d // n_heads
    alpha = float((2 * depth) ** 0.25)
    length = n_meta + seq
    lp = -(-length // SEQ_TILE) * SEQ_TILE
    assert d % (len(POOL_WINDOWS) * LANES) == 0 and head_dim % 16 == 0 and head_dim + 16 <= LANES and depth == 2

    meta = jnp.broadcast_to(meta_tokens[None], (bsz, n_meta, d))
    h = jnp.concatenate([meta, x, jnp.zeros((bsz, lp - length, d), x.dtype)], axis=1)
    h = h.reshape(bsz * lp, d)

    h, h_tiles, top_i, gates, rank, counts = _pool_call(
        h, pool_w[0], pool_scale[0], ln_g[0, 0], ln_b[0, 0], router_w[0], router_b[0],
        bsz=bsz, lp=lp, alpha=alpha)
    h = _moe(h, h_tiles, top_i, gates, rank, counts, 0, w1, b1, w2, b2, ln_g[0, 1], ln_b[0, 1], alpha=alpha)

    weights = _attn_weights(attn_w_in[0], attn_b_f[0], n_heads, head_dim)
    qt, kx, vt = _proj_call(h, weights, bsz=bsz, lp=lp, n_heads=n_heads, head_dim=head_dim)
    o = _attn_call(qt, kx, vt, head_dim=head_dim)
    h, h_tiles, top_i, gates, rank, counts = _oproj_call(
        o, attn_w_out[0], h, ln_g[1, 0], ln_b[1, 0], router_w[1], router_b[1],
        bsz=bsz, lp=lp, alpha=alpha)
    h = _moe(h, h_tiles, top_i, gates, rank, counts, 1, w1, b1, w2, b2, ln_g[1, 1], ln_b[1, 1], alpha=alpha)

    return h.reshape(bsz, lp, d)[:, n_meta:length]
```

```python
import functools

import numpy as np
import jax
import jax.numpy as jnp
from jax import lax
from jax.experimental import pallas as pl
from jax.experimental.pallas import tpu as pltpu

POOL_WINDOWS = (2, 4, 8, 16)
MAX_WIN = max(POOL_WINDOWS)
TOP_K = 4
SWIGLU_LIMIT = 7.0
SWIGLU_ALPHA = 1.702
LN_EPS = 1e-5
MASK_VALUE = -1e30

LANES = 128
SEQ_TILE = 512
ROUTE_TILE = 512
EXPERT_ROWS = 512
VMEM_LIMIT = 56 * 1024 * 1024
LOG2E = 1.4426950408889634
ATTN_DEPTH = 3

F32 = jnp.float32
BF16 = jnp.bfloat16


def _params(sem, vmem=VMEM_LIMIT):
    return pltpu.CompilerParams(dimension_semantics=sem, vmem_limit_bytes=vmem)


def _dot3(a, b):
    a_hi = a.astype(BF16)
    a_lo = (a - a_hi.astype(F32)).astype(BF16)
    b_hi = b.astype(BF16)
    b_lo = (b - b_hi.astype(F32)).astype(BF16)
    dot = functools.partial(jnp.dot, preferred_element_type=F32)
    return dot(a_hi, b_hi) + dot(a_hi, b_lo) + dot(a_lo, b_hi)


def _layer_norm(z, g, b):
    mu = jnp.mean(z, axis=-1, keepdims=True)
    d = z - mu
    var = jnp.mean(d * d, axis=-1, keepdims=True)
    return d * lax.rsqrt(var + LN_EPS) * g + b


def _route(hn, rw_ref, rb_ref, run_ref, ti_ref, gate_ref, rank_ref, cnt_ref):
    rows = hn.shape[0]
    logits = _dot3(hn, rw_ref[...]) + rb_ref[...]
    n_exp = logits.shape[1]
    lane = lax.broadcasted_iota(jnp.int32, (rows, n_exp), 1)
    cur = logits
    vals, idxs, hots = [], [], []
    for _ in range(TOP_K):
        m = jnp.max(cur, axis=-1, keepdims=True)
        idx = jnp.min(jnp.where(cur == m, lane, n_exp), axis=-1, keepdims=True)
        hot = lane == idx
        vals.append(m)
        idxs.append(idx)
        hots.append(hot)
        cur = jnp.where(hot, -jnp.inf, cur)
    exps = [jnp.exp(v - vals[0]) for v in vals]
    denom = exps[0]
    for e in exps[1:]:
        denom = denom + e
    gates = [e / denom for e in exps]
    sel = hots[0].astype(F32)
    for hot in hots[1:]:
        sel = sel + hot.astype(F32)
    r = lax.broadcasted_iota(jnp.int32, (rows, rows), 0)
    c = lax.broadcasted_iota(jnp.int32, (rows, rows), 1)
    tri = (c < r).astype(BF16)
    base = run_ref[...] + jnp.dot(tri, sel.astype(BF16), preferred_element_type=F32)
    ranks = [jnp.sum(jnp.where(hot, base, 0.0), axis=-1, keepdims=True) for hot in hots]
    run_ref[...] = run_ref[...] + jnp.sum(sel, axis=0, keepdims=True)
    cnt_ref[...] = run_ref[...]
    lane_k = lax.broadcasted_iota(jnp.int32, (rows, TOP_K), 1)

    def pack(cols):
        out = jnp.broadcast_to(cols[TOP_K - 1], (rows, TOP_K))
        for k in range(TOP_K - 2, -1, -1):
            out = jnp.where(lane_k == k, cols[k], out)
        return out

    ti_ref[...] = pack(idxs)
    gate_ref[...] = pack(gates)
    rank_ref[...] = pack(ranks).astype(jnp.int32)


def _pool_kernel(h_ref, pw_ref, ps_ref, g_ref, b_ref, rw_ref, rb_ref,
                 h1_ref, ti_ref, gate_ref, rank_ref, cnt_ref,
                 ext_ref, run_ref, *, tile, alpha):
    bi = pl.program_id(0)
    i = pl.program_id(1)
    d_model = h_ref.shape[1]
    gdim = d_model // len(POOL_WINDOWS)

    @pl.when(i == 0)
    def _():
        ext_ref[0:MAX_WIN, :] = jnp.zeros((MAX_WIN, d_model), F32)

    @pl.when((bi == 0) & (i == 0))
    def _():
        run_ref[...] = jnp.zeros_like(run_ref)

    x = h_ref[...]
    ext_ref[MAX_WIN:MAX_WIN + tile, :] = x
    pos = i * tile + lax.broadcasted_iota(jnp.int32, (tile, 1), 0)
    ys = []
    for g, w in enumerate(POOL_WINDOWS):
        lo, hi = g * gdim, (g + 1) * gdim
        xg = x[:, lo:hi]
        s = xg
        for j in range(1, w):
            s = s + ext_ref[MAX_WIN - j:MAX_WIN - j + tile, lo:hi]
        cnt = jnp.minimum(pos + 1, w).astype(F32)
        u = s / cnt - xg
        ys.append(jnp.dot(u.astype(BF16), pw_ref[g].astype(BF16), preferred_element_type=F32))
    y = jnp.concatenate(ys, axis=-1) * ps_ref[...]
    hn = _layer_norm(alpha * x + y, g_ref[...], b_ref[...])
    h1_ref[...] = hn
    ext_ref[0:MAX_WIN, :] = x[tile - MAX_WIN:tile, :]
    _route(hn, rw_ref, rb_ref, run_ref, ti_ref, gate_ref, rank_ref, cnt_ref)


def _route_out_shapes(tp, n_exp):
    return [jax.ShapeDtypeStruct((tp, TOP_K), jnp.int32),
            jax.ShapeDtypeStruct((tp, TOP_K), F32),
            jax.ShapeDtypeStruct((tp, TOP_K), jnp.int32),
            jax.ShapeDtypeStruct((1, n_exp), F32)]


def _route_out_specs(nt, tile, n_exp):
    row = lambda b, i: (b * nt + i, 0)
    return [pl.BlockSpec((tile, TOP_K), row),
            pl.BlockSpec((tile, TOP_K), row),
            pl.BlockSpec((tile, TOP_K), row),
            pl.BlockSpec((1, n_exp), lambda b, i: (0, 0))]


def _const_spec(shape):
    return pl.BlockSpec(shape, lambda b, i: (0,) * len(shape))


def _pool_call(h, pool_w, pool_scale, ln_g, ln_b, router_w, router_b, *, bsz, lp, alpha):
    tp, d = h.shape
    tile = SEQ_TILE
    nt = lp // tile
    n_exp = router_w.shape[1]
    groups, gdim, _ = pool_w.shape
    row = lambda b, i: (b * nt + i, 0)
    return pl.pallas_call(
        functools.partial(_pool_kernel, tile=tile, alpha=alpha),
        grid=(bsz, nt),
        in_specs=[pl.BlockSpec((tile, d), row),
                  _const_spec((groups, gdim, gdim)),
                  _const_spec((1, d)), _const_spec((1, d)), _const_spec((1, d)),
                  _const_spec((d, n_exp)), _const_spec((1, n_exp))],
        out_specs=[pl.BlockSpec((tile, d), row)] + _route_out_specs(nt, tile, n_exp),
        out_shape=[jax.ShapeDtypeStruct((tp, d), F32)] + _route_out_shapes(tp, n_exp),
        scratch_shapes=[pltpu.VMEM((MAX_WIN + tile, d), F32), pltpu.VMEM((1, n_exp), F32)],
        compiler_params=_params(("arbitrary", "arbitrary")),
    )(h, pool_w, pool_scale.reshape(1, d), ln_g.reshape(1, d), ln_b.reshape(1, d),
      router_w, router_b.reshape(1, n_exp))


def _plan(counts, top_i, rank, n_blocks):
    tp = top_i.shape[0]
    n_exp = counts.shape[1]
    n_rows = n_blocks * EXPERT_ROWS
    cnt = counts.reshape(n_exp).astype(jnp.int32)
    padded = (cnt + EXPERT_ROWS - 1) // EXPERT_ROWS * EXPERT_ROWS
    pad_ends = jnp.cumsum(padded)
    pad_starts = pad_ends - padded
    hot = top_i[..., None] == jnp.arange(n_exp, dtype=jnp.int32)
    dest = (jnp.sum(jnp.where(hot, pad_starts, 0), axis=-1) + rank).reshape(-1)
    starts = jnp.arange(n_blocks, dtype=jnp.int32) * EXPERT_ROWS
    block_e = jnp.sum((starts[:, None] >= pad_ends[None, :]).astype(jnp.int32), axis=1)
    block_e = jnp.minimum(block_e, n_exp - 1)
    n_used = (pad_ends[-1] // EXPERT_ROWS).reshape(1)
    tails = jnp.minimum((pad_starts + cnt) // 8 * 8, n_rows - EXPERT_ROWS)

    copy = jnp.arange(tp * TOP_K, dtype=jnp.int32)
    row_copy = jnp.full((n_rows,), -1, jnp.int32).at[dest].set(copy, unique_indices=True)
    valid = row_copy >= 0
    tok, k = row_copy // TOP_K, row_copy % TOP_K
    scrap = tp * TOP_K + jnp.cumsum(jnp.logical_not(valid).astype(jnp.int32)) - 1
    row_dst = jnp.where(valid, k * tp + tok, scrap)
    first = n_rows + jnp.arange(EXPERT_ROWS, dtype=jnp.int32) % (EXPERT_ROWS // 2)
    dst_ext = jnp.concatenate([first, row_dst])
    return (block_e, n_used, tails, dest.reshape(-1, 1, SEQ_TILE * TOP_K),
            dst_ext.reshape(n_blocks + 1, 1, EXPERT_ROWS))


def _dispatch_kernel(nu_ref, tails_ref, dest_ref, h_ref, xs_ref, zero_ref, sem, *, n_blocks):
    i = pl.program_id(0)
    tile = h_ref.shape[0]
    rb = zero_ref.shape[0]

    @pl.when(i == 0)
    def _():
        zero_ref[...] = jnp.zeros_like(zero_ref)
        fills = [pltpu.make_async_copy(zero_ref, xs_ref.at[pl.ds(pl.multiple_of(tails_ref[e], 8), rb), :], sem)
                 for e in range(tails_ref.shape[0])]
        for cp in fills:
            cp.start()
        for cp in fills:
            cp.wait()

        def fill_block(blk, carry):
            cp = pltpu.make_async_copy(zero_ref, xs_ref.at[pl.ds(pl.multiple_of(blk * rb, rb), rb), :], sem)
            cp.start()
            cp.wait()
            return carry

        lax.fori_loop(nu_ref[0], n_blocks, fill_block, 0)

    for r in range(tile):
        for k in range(TOP_K):
            pltpu.make_async_copy(h_ref.at[pl.ds(r, 1), :],
                                  xs_ref.at[pl.ds(dest_ref[0, 0, r * TOP_K + k], 1), :], sem).start()
    for _ in range(TOP_K):
        pltpu.make_async_copy(h_ref, xs_ref.at[pl.ds(0, tile), :], sem).wait()


def _dispatch_call(n_used, tails, dest, h, n_blocks):
    tp, d = h.shape
    tile = SEQ_TILE
    rb = EXPERT_ROWS
    grid_spec = pltpu.PrefetchScalarGridSpec(
        num_scalar_prefetch=2,
        grid=(tp // tile,),
        in_specs=[pl.BlockSpec((1, 1, tile * TOP_K), lambda i, nu, tl: (i, 0, 0), memory_space=pltpu.SMEM),
                  pl.BlockSpec((tile, d), lambda i, nu, tl: (i, 0))],
        out_specs=pl.BlockSpec(memory_space=pl.ANY),
        scratch_shapes=[pltpu.VMEM((rb, d), F32), pltpu.SemaphoreType.DMA(())],
    )
    return pl.pallas_call(
        functools.partial(_dispatch_kernel, n_blocks=n_blocks),
        grid_spec=grid_spec,
        out_shape=jax.ShapeDtypeStruct((n_blocks * rb, d), F32),
        compiler_params=_params(("arbitrary",)),
    )(n_used, tails, dest, h)


def _ffn_kernel(be_ref, nu_ref, dstp_ref, dst_ref, x_ref, w1_ref, b1_ref, w2_ref, b2_ref,
                y_ref, ya_ref, yb_ref, w1b_ref, w2b_ref, ssem):
    i = pl.program_id(0)
    half = ya_ref.shape[0]
    f = w2b_ref.shape[0]

    def scatter(idx_ref, base, buf_ref, sem):
        for r in range(half):
            pltpu.make_async_copy(buf_ref.at[pl.ds(r, 1), :],
                                  y_ref.at[pl.ds(idx_ref[0, 0, base + r], 1), :], sem).start()

    def wait_scatter(buf_ref, sem):
        pltpu.make_async_copy(buf_ref, y_ref.at[pl.ds(0, half), :], sem).wait()

    def ffn(x, out_ref):
        hid = jnp.dot(x.astype(BF16), w1b_ref[...], preferred_element_type=F32) + b1_ref[0]
        gate = jnp.minimum(hid[:, :f], SWIGLU_LIMIT)
        up = jnp.clip(hid[:, f:], -SWIGLU_LIMIT, SWIGLU_LIMIT)
        act = gate * jax.nn.sigmoid(SWIGLU_ALPHA * gate) * (up + 1.0)
        out_ref[...] = jnp.dot(act.astype(BF16), w2b_ref[...], preferred_element_type=F32) + b2_ref[0]

    @pl.when(i < nu_ref[0])
    def _():
        @pl.when(i == 0)
        def _():
            yb_ref[...] = jnp.zeros_like(yb_ref)

        @pl.when((i == 0) | (be_ref[i] != be_ref[jnp.maximum(i - 1, 0)]))
        def _():
            w1b_ref[...] = w1_ref[0, 0].astype(BF16)
            w2b_ref[...] = w2_ref[0, 0].astype(BF16)

        @pl.when(i >= 1)
        def _():
            wait_scatter(ya_ref, ssem.at[0])

        scatter(dstp_ref, half, yb_ref, ssem.at[1])
        ffn(x_ref[0:half, :], ya_ref)

        wait_scatter(yb_ref, ssem.at[1])
        scatter(dst_ref, 0, ya_ref, ssem.at[0])
        ffn(x_ref[half:2 * half, :], yb_ref)

        @pl.when(i == nu_ref[0] - 1)
        def _():
            scatter(dst_ref, half, yb_ref, ssem.at[1])
            wait_scatter(ya_ref, ssem.at[0])
            wait_scatter(yb_ref, ssem.at[1])

    @pl.when(i >= nu_ref[0])
    def _():
        ya_ref[...] = jnp.zeros_like(ya_ref)
        first = pl.multiple_of(dst_ref[0, 0, 0], 8)
        copies = [pltpu.make_async_copy(ya_ref, y_ref.at[pl.ds(first + j * half, half), :], ssem.at[j])
                  for j in range(2)]
        for cp in copies:
            cp.start()
        for cp in copies:
            cp.wait()


def _ffn_call(block_e, n_used, dst_ext, xs, layer, w1, b1, w2, b2):
    n_rows, d = xs.shape
    depth, n_exp, _, f2 = w1.shape
    f = w2.shape[2]
    rb = EXPERT_ROWS
    nb = n_rows // rb
    half = rb // 2
    n_out = n_rows + half

    def blk(i, be, nu):
        return jnp.minimum(i, nu[0] - 1)

    def idx_spec(index_map):
        return pl.BlockSpec((1, 1, rb), index_map, memory_space=pltpu.SMEM)

    grid_spec = pltpu.PrefetchScalarGridSpec(
        num_scalar_prefetch=2,
        grid=(nb,),
        in_specs=[idx_spec(lambda i, be, nu: (i, 0, 0)),
                  idx_spec(lambda i, be, nu: (i + 1, 0, 0)),
                  pl.BlockSpec((rb, d), lambda i, be, nu: (blk(i, be, nu), 0)),
                  pl.BlockSpec((1, 1, d, f2), lambda i, be, nu: (layer, be[blk(i, be, nu)], 0, 0)),
                  pl.BlockSpec((1, 1, f2), lambda i, be, nu: (layer * n_exp + be[blk(i, be, nu)], 0, 0)),
                  pl.BlockSpec((1, 1, f, d), lambda i, be, nu: (layer, be[blk(i, be, nu)], 0, 0)),
                  pl.BlockSpec((1, 1, d), lambda i, be, nu: (layer * n_exp + be[blk(i, be, nu)], 0, 0))],
        out_specs=pl.BlockSpec(memory_space=pl.ANY),
        scratch_shapes=[pltpu.VMEM((half, d), F32)] * 2 + [
            pltpu.VMEM((d, f2), BF16), pltpu.VMEM((f, d), BF16), pltpu.SemaphoreType.DMA((2,))],
    )
    return pl.pallas_call(
        _ffn_kernel,
        grid_spec=grid_spec,
        out_shape=jax.ShapeDtypeStruct((n_out, d), F32),
        compiler_params=_params(("arbitrary",)),
    )(block_e, n_used, dst_ext, dst_ext, xs,
      w1, b1.reshape(depth * n_exp, 1, f2), w2, b2.reshape(depth * n_exp, 1, d))


def _combine_kernel(gate_ref, h_ref, g_ref, b_ref, *refs, alpha):
    y_refs, out_ref = refs[:TOP_K], refs[TOP_K]
    gates = gate_ref[...]
    ffn = gates[:, 0:1] * y_refs[0][...]
    for k in range(1, TOP_K):
        ffn = ffn + gates[:, k:k + 1] * y_refs[k][...]
    out_ref[...] = _layer_norm(alpha * h_ref[...] + ffn, g_ref[...], b_ref[...])


def _combine_call(gates, h, y, ln_g, ln_b, *, alpha):
    tp, d = h.shape
    tile = ROUTE_TILE
    nt = tp // tile
    y_specs = [pl.BlockSpec((tile, d), functools.partial(lambda k, i: (k * nt + i, 0), k)) for k in range(TOP_K)]
    return pl.pallas_call(
        functools.partial(_combine_kernel, alpha=alpha),
        grid=(nt,),
        in_specs=[pl.BlockSpec((tile, TOP_K), lambda i: (i, 0)),
                  pl.BlockSpec((tile, d), lambda i: (i, 0)),
                  pl.BlockSpec((1, d), lambda i: (0, 0)),
                  pl.BlockSpec((1, d), lambda i: (0, 0))] + y_specs,
        out_specs=pl.BlockSpec((tile, d), lambda i: (i, 0)),
        out_shape=jax.ShapeDtypeStruct((tp, d), F32),
        compiler_params=_params(("arbitrary",)),
    )(gates, h, ln_g.reshape(1, d), ln_b.reshape(1, d), *([y] * TOP_K))


def _moe(h, top_i, gates, rank, counts, layer, w1, b1, w2, b2, ln_g, ln_b, *, alpha):
    tp = h.shape[0]
    n_exp = w1.shape[1]
    n_blocks = tp * TOP_K // EXPERT_ROWS + n_exp
    block_e, n_used, tails, dest, dst_ext = _plan(counts, top_i, rank, n_blocks)
    xs = _dispatch_call(n_used, tails, dest, h, n_blocks)
    y = _ffn_call(block_e, n_used, dst_ext, xs, layer, w1, b1, w2, b2)
    return _combine_call(gates, h, y, ln_g, ln_b, alpha=alpha)


def _log_sigmoid(x):
    return jnp.minimum(x, 0.0) - jnp.log(1.0 + jnp.exp(-jnp.abs(x)))


def _split3(c):
    hi = c.astype(BF16)
    r1 = c - hi.astype(F32)
    mid = r1.astype(BF16)
    lo = (r1 - mid.astype(F32)).astype(BF16)
    return hi, mid, lo


def _proj_kernel(h_ref, wqt_ref, wk_ref, wvt_ref, wf_ref, bf_ref, selk_ref, onek_ref,
                 qt_ref, kx_ref, vt_ref, carry_ref, *, tile, n_heads, head_dim):
    i = pl.program_id(1)

    @pl.when(i == 0)
    def _():
        carry_ref[...] = jnp.zeros_like(carry_ref)

    x = h_ref[...]
    xb = x.astype(BF16)
    nt_dims = (((1,), (1,)), ((), ()))
    lf = _log_sigmoid(_dot3(x, wf_ref[...]) + bf_ref[...])
    r = lax.broadcasted_iota(jnp.int32, (tile, tile), 0)
    c = lax.broadcasted_iota(jnp.int32, (tile, tile), 1)
    tril = (c <= r).astype(BF16)
    cs = carry_ref[...]
    for part in _split3(lf):
        cs = cs + jnp.dot(tril, part, preferred_element_type=F32)
    carry_ref[...] = cs[tile - 1:tile, :]
    parts = _split3(cs * LOG2E)
    eye = (r == c).astype(BF16)
    parts_t = [lax.dot_general(p, eye, (((0,), (0,)), ((), ())), preferred_element_type=F32)
               for p in parts]

    kx = jnp.dot(xb, wk_ref[...], preferred_element_type=F32) + onek_ref[...]
    for p, part in enumerate(parts):
        kx = kx + jnp.dot(-part, selk_ref[p], preferred_element_type=F32)
    qt = lax.dot_general(wqt_ref[...], xb, nt_dims, preferred_element_type=F32)
    vt = lax.dot_general(wvt_ref[...], xb, nt_dims, preferred_element_type=F32)
    aug_rows = 16
    row = lax.broadcasted_iota(jnp.int32, (aug_rows, tile), 0)
    v_aug = jnp.where(row == 0, 1.0, 0.0).astype(BF16)
    rest = jnp.zeros((LANES - head_dim - aug_rows, tile), BF16)
    for hd in range(n_heads):
        lo = hd * head_dim
        q_aug = jnp.where(row < 6, 1.0, 0.0)
        for p in range(2, -1, -1):
            q_aug = jnp.where(row == p, parts_t[p][hd:hd + 1, :], q_aug)
        qt_ref[0, hd, 0, 0:head_dim, :] = qt[lo:lo + head_dim, :].astype(BF16)
        qt_ref[0, hd, 0, head_dim:head_dim + aug_rows, :] = q_aug.astype(BF16)
        qt_ref[0, hd, 0, head_dim + aug_rows:, :] = rest
        vt_ref[0, hd, 0, 0:head_dim, :] = vt[lo:lo + head_dim, :].astype(BF16)
        vt_ref[0, hd, 0, head_dim:head_dim + aug_rows, :] = v_aug
        vt_ref[0, hd, 0, head_dim + aug_rows:, :] = rest
        kx_ref[0, hd, 0] = kx[:, hd * LANES:(hd + 1) * LANES].astype(BF16)


def _attn_weights(w_in, b_f, n_heads, head_dim):
    d = w_in.shape[0]
    scale = head_dim ** -0.5 * LOG2E
    hw = n_heads * LANES
    wk = w_in[:, d:2 * d].reshape(d, n_heads, head_dim)
    wk = jnp.pad(wk, ((0, 0), (0, 0), (0, LANES - head_dim))).reshape(d, hw).astype(BF16)
    wqt = (w_in[:, :d] * scale).T.astype(BF16)
    wvt = w_in[:, 2 * d:3 * d].T.astype(BF16)
    selk = np.zeros((3, n_heads, hw), np.float32)
    onek = np.zeros((1, hw), np.float32)
    for h in range(n_heads):
        base = h * LANES + head_dim
        for p in range(3):
            selk[p, h, base + 3 + p] = 1.0
            onek[0, base + p] = 1.0
    return (wqt, wk, wvt, w_in[:, 3 * d:], b_f.reshape(1, n_heads), jnp.asarray(selk, BF16), jnp.asarray(onek))


def _proj_call(h, weights, *, bsz, lp, n_heads, head_dim):
    tp, d = h.shape
    tile = SEQ_TILE
    nt = lp // tile
    row = lambda b, i: (b * nt + i, 0)
    in_specs = [pl.BlockSpec((tile, d), row)] + [_const_spec(w.shape) for w in weights]
    t_spec = pl.BlockSpec((1, n_heads, 1, LANES, tile), lambda b, i: (b, 0, i, 0, 0))
    k_spec = pl.BlockSpec((1, n_heads, 1, tile, LANES), lambda b, i: (b, 0, i, 0, 0))
    return pl.pallas_call(
        functools.partial(_proj_kernel, tile=tile, n_heads=n_heads, head_dim=head_dim),
        grid=(bsz, nt),
        in_specs=in_specs,
        out_specs=[t_spec, k_spec, t_spec],
        out_shape=[jax.ShapeDtypeStruct((bsz, n_heads, nt, LANES, tile), BF16),
                   jax.ShapeDtypeStruct((bsz, n_heads, nt, tile, LANES), BF16),
                   jax.ShapeDtypeStruct((bsz, n_heads, nt, LANES, tile), BF16)],
        scratch_shapes=[pltpu.VMEM((1, n_heads), F32)],
        compiler_params=_params(("arbitrary", "arbitrary")),
    )(h, *weights)


def _attn_kernel(qt_ref, kx_ref, vt_ref, o_ref, s0_ref, s1_ref, s2_ref, p0_ref, p1_ref, p2_ref,
                 a0_ref, a1_ref, a2_ref, c0_ref, c1_ref, c2_ref, m_ref, acc_ref, *, nt, tile, head_dim):
    s_refs, p_refs, a_refs = (s0_ref, s1_ref, s2_ref), (p0_ref, p1_ref, p2_ref), (a0_ref, a1_ref, a2_ref)
    c_refs = (c0_ref, c1_ref, c2_ref)
    key = lax.broadcasted_iota(jnp.int32, (tile, tile), 0)
    qry = lax.broadcasted_iota(jnp.int32, (tile, tile), 1)

    def logits(slot, qi, kj):
        s = jnp.dot(kx_ref[0, 0, kj], qt_ref[0, 0, qi], preferred_element_type=F32)
        s_refs[slot][...] = s
        c_refs[slot][...] = jnp.max(s, axis=0, keepdims=True)

    def value_update(slot, kj, out_qi=None):
        acc = a_refs[slot][...] * acc_ref[...] + jnp.dot(
            vt_ref[0, 0, kj], p_refs[slot][...], preferred_element_type=F32)
        acc_ref[...] = acc
        if out_qi is not None:
            o_ref[0, 0, out_qi] = (acc[:head_dim, :] * (1.0 / acc[head_dim:head_dim + 1, :])).astype(BF16)

    def substep(slot, qi, kj, *, diag, write_out):
        if diag:
            nqi, nkj = qi + 1, jnp.int32(1)
        else:
            stay = qi - kj >= 2
            nqi, nkj = jnp.where(stay, qi, qi + 1), jnp.where(stay, kj + 2, 0)
        logits((slot + 2) % ATTN_DEPTH, jnp.minimum(nqi, nt - 1), nkj)
        pkj = jnp.where(kj >= 2, kj - 2, jnp.maximum(qi - 1 - jnp.where(kj == 0, 1, 0), 0))
        out_qi = jnp.where(kj == 1, qi - 1, qi) if write_out else None
        value_update((slot + 1) % ATTN_DEPTH, pkj, out_qi)
        m_old = m_ref[...]
        if diag:
            s = jnp.where(key <= qry, s_refs[slot][...], MASK_VALUE)
            m_new = jnp.maximum(m_old, jnp.max(s, axis=0, keepdims=True))
            p_refs[slot][...] = jnp.exp2(s - m_new).astype(BF16)
            m_ref[...] = jnp.full_like(m_old, MASK_VALUE)
        else:
            m_new = jnp.maximum(m_old, c_refs[slot][...])
            p_refs[slot][...] = jnp.exp2(s_refs[slot][...] - m_new).astype(BF16)
            m_ref[...] = m_new
        a_refs[slot][...] = jnp.exp2(m_old - m_new)

    def q_tile(slot, qi, n_loops, rem):
        def trip(i, carry):
            for j in range(ATTN_DEPTH):
                substep((slot + j) % ATTN_DEPTH, qi, ATTN_DEPTH * i + j, diag=False, write_out=j == 1)
            return carry

        lax.fori_loop(0, n_loops, trip, 0)
        for j in range(rem):
            substep((slot + j) % ATTN_DEPTH, qi, ATTN_DEPTH * n_loops + j, diag=False, write_out=j == 1)
        substep((slot + rem) % ATTN_DEPTH, qi, qi, diag=True, write_out=rem == 1)
        return (slot + rem + 1) % ATTN_DEPTH

    m_ref[...] = jnp.full_like(m_ref, MASK_VALUE)
    acc_ref[...] = jnp.ones_like(acc_ref)
    for slot in range(1, ATTN_DEPTH):
        p_refs[slot][...] = jnp.zeros_like(p_refs[slot])
        a_refs[slot][...] = jnp.ones_like(a_refs[slot])
    logits(0, 0, 0)
    logits(1, 1, 0)

    def group(g, carry):
        slot = 0
        for r in range(ATTN_DEPTH):
            slot = q_tile(slot, ATTN_DEPTH * g + r, g, r)
        assert slot == 0
        return carry

    lax.fori_loop(0, nt // ATTN_DEPTH, group, 0)
    slot = 0
    for qi in range(nt // ATTN_DEPTH * ATTN_DEPTH, nt):
        slot = q_tile(slot, jnp.int32(qi), jnp.int32(qi // ATTN_DEPTH), qi % ATTN_DEPTH)
    value_update((slot + 1) % ATTN_DEPTH, nt - 2)
    value_update((slot + 2) % ATTN_DEPTH, nt - 1, nt - 1)


def _attn_call(qt, kx, vt, *, head_dim):
    bsz, n_heads, nt, _, tile = qt.shape
    return pl.pallas_call(
        functools.partial(_attn_kernel, nt=nt, tile=tile, head_dim=head_dim),
        grid=(bsz, n_heads),
        in_specs=[pl.BlockSpec((1, 1, nt, LANES, tile), lambda b, h: (b, h, 0, 0, 0)),
                  pl.BlockSpec((1, 1, nt, tile, LANES), lambda b, h: (b, h, 0, 0, 0)),
                  pl.BlockSpec((1, 1, nt, LANES, tile), lambda b, h: (b, h, 0, 0, 0))],
        out_specs=pl.BlockSpec((1, 1, nt, head_dim, tile), lambda b, h: (b, h, 0, 0, 0)),
        out_shape=jax.ShapeDtypeStruct((bsz, n_heads, nt, head_dim, tile), BF16),
        scratch_shapes=([pltpu.VMEM((tile, tile), F32)] * ATTN_DEPTH + [pltpu.VMEM((tile, tile), BF16)] * ATTN_DEPTH
                        + [pltpu.VMEM((1, tile), F32)] * (2 * ATTN_DEPTH)
                        + [pltpu.VMEM((1, tile), F32), pltpu.VMEM((LANES, tile), F32)]),
        compiler_params=_params(("arbitrary", "arbitrary")),
    )(qt, kx, vt)


def _oproj_kernel(o_ref, wo_ref, h_ref, g_ref, b_ref, rw_ref, rb_ref,
                  hn_ref, ti_ref, gate_ref, rank_ref, cnt_ref, run_ref, *, alpha):
    bi = pl.program_id(0)
    i = pl.program_id(1)

    @pl.when((bi == 0) & (i == 0))
    def _():
        run_ref[...] = jnp.zeros_like(run_ref)

    n_heads, _, head_dim, tile = o_ref.shape[1:]
    o_t = o_ref[0].reshape(n_heads * head_dim, tile)
    att = lax.dot_general(o_t, wo_ref[...], (((0,), (0,)), ((), ())), preferred_element_type=F32)
    hn = _layer_norm(alpha * h_ref[...] + att, g_ref[...], b_ref[...])
    hn_ref[...] = hn
    _route(hn, rw_ref, rb_ref, run_ref, ti_ref, gate_ref, rank_ref, cnt_ref)


def _oproj_call(o, w_out, h, ln_g, ln_b, router_w, router_b, *, bsz, lp, alpha):
    tp, d = h.shape
    tile = SEQ_TILE
    nt = lp // tile
    n_exp = router_w.shape[1]
    row = lambda b, i: (b * nt + i, 0)
    return pl.pallas_call(
        functools.partial(_oproj_kernel, alpha=alpha),
        grid=(bsz, nt),
        in_specs=[pl.BlockSpec((1,) + o.shape[1:2] + (1,) + o.shape[3:], lambda b, i: (b, 0, i, 0, 0)),
                  _const_spec((d, d)),
                  pl.BlockSpec((tile, d), row),
                  _const_spec((1, d)), _const_spec((1, d)),
                  _const_spec((d, n_exp)), _const_spec((1, n_exp))],
        out_specs=[pl.BlockSpec((tile, d), row)] + _route_out_specs(nt, tile, n_exp),
        out_shape=[jax.ShapeDtypeStruct((tp, d), F32)] + _route_out_shapes(tp, n_exp),
        scratch_shapes=[pltpu.VMEM((1, n_exp), F32)],
        compiler_params=_params(("arbitrary", "arbitrary")),
    )(o, w_out.astype(BF16), h, ln_g.reshape(1, d), ln_b.reshape(1, d),
      router_w, router_b.reshape(1, n_exp))


def kernel(x, meta_tokens, pool_w, pool_scale, attn_w_in, attn_b_f, attn_w_out,
           ln_g, ln_b, router_w, router_b, w1, b1, w2, b2):
    bsz, seq, d = x.shape
    n_meta = meta_tokens.shape[0]
    depth = ln_g.shape[0]
    n_heads = attn_b_f.shape[-1]
    head_dim = d // n_heads
    alpha = float((2 * depth) ** 0.25)
    length = n_meta + seq
    lp = -(-length // SEQ_TILE) * SEQ_TILE
    assert d % (len(POOL_WINDOWS) * LANES) == 0 and head_dim % 16 == 0 and head_dim + 16 <= LANES and depth == 2

    meta = jnp.broadcast_to(meta_tokens[None], (bsz, n_meta, d))
    h = jnp.concatenate([meta, x, jnp.zeros((bsz, lp - length, d), x.dtype)], axis=1)
    h = h.reshape(bsz * lp, d)

    h, top_i, gates, rank, counts = _pool_call(
        h, pool_w[0], pool_scale[0], ln_g[0, 0], ln_b[0, 0], router_w[0], router_b[0],
        bsz=bsz, lp=lp, alpha=alpha)
    h = _moe(h, top_i, gates, rank, counts, 0, w1, b1, w2, b2, ln_g[0, 1], ln_b[0, 1], alpha=alpha)

    weights = _attn_weights(attn_w_in[0], attn_b_f[0], n_heads, head_dim)
    qt, kx, vt = _proj_call(h, weights, bsz=bsz, lp=lp, n_heads=n_heads, head_dim=head_dim)
    o = _attn_call(qt, kx, vt, head_dim=head_dim)
    h, top_i, gates, rank, counts = _oproj_call(
        o, attn_w_out[0], h, ln_g[1, 0], ln_b[1, 0], router_w[1], router_b[1],
        bsz=bsz, lp=lp, alpha=alpha)
    h = _moe(h, top_i, gates, rank, counts, 1, w1, b1, w2, b2, ln_g[1, 1], ln_b[1, 1], alpha=alpha)

    return h.reshape(bsz, lp, d)[:, n_meta:length]
```

```python
import functools

import numpy as np
import jax
import jax.numpy as jnp
from jax import lax
from jax.experimental import pallas as pl
from jax.experimental.pallas import tpu as pltpu

POOL_WINDOWS = (2, 4, 8, 16)
MAX_WIN = max(POOL_WINDOWS)
TOP_K = 4
SWIGLU_LIMIT = 7.0
SWIGLU_ALPHA = 1.702
LN_EPS = 1e-5
MASK_VALUE = -1e30

LANES = 128
SEQ_TILE = 512
ROUTE_TILE = 512
EXPERT_ROWS = 512
VMEM_LIMIT = 56 * 1024 * 1024
LOG2E = 1.4426950408889634
ATTN_DEPTH = 3

F32 = jnp.float32
BF16 = jnp.bfloat16


def _params(sem, vmem=VMEM_LIMIT):
    return pltpu.CompilerParams(dimension_semantics=sem, vmem_limit_bytes=vmem)


def _dot3(a, b):
    a_hi = a.astype(BF16)
    a_lo = (a - a_hi.astype(F32)).astype(BF16)
    b_hi = b.astype(BF16)
    b_lo = (b - b_hi.astype(F32)).astype(BF16)
    dot = functools.partial(jnp.dot, preferred_element_type=F32)
    return dot(a_hi, b_hi) + dot(a_hi, b_lo) + dot(a_lo, b_hi)


def _layer_norm(z, g, b):
    mu = jnp.mean(z, axis=-1, keepdims=True)
    d = z - mu
    var = jnp.mean(d * d, axis=-1, keepdims=True)
    return d * lax.rsqrt(var + LN_EPS) * g + b


def _route(hn, rw_ref, rb_ref, run_ref, ti_ref, gate_ref, rank_ref, cnt_ref):
    rows = hn.shape[0]
    logits = _dot3(hn, rw_ref[...]) + rb_ref[...]
    n_exp = logits.shape[1]
    lane = lax.broadcasted_iota(jnp.int32, (rows, n_exp), 1)
    cur = logits
    vals, idxs, hots = [], [], []
    for _ in range(TOP_K):
        m = jnp.max(cur, axis=-1, keepdims=True)
        idx = jnp.min(jnp.where(cur == m, lane, n_exp), axis=-1, keepdims=True)
        hot = lane == idx
        vals.append(m)
        idxs.append(idx)
        hots.append(hot)
        cur = jnp.where(hot, -jnp.inf, cur)
    exps = [jnp.exp(v - vals[0]) for v in vals]
    denom = exps[0]
    for e in exps[1:]:
        denom = denom + e
    gates = [e / denom for e in exps]
    sel = hots[0].astype(F32)
    for hot in hots[1:]:
        sel = sel + hot.astype(F32)
    r = lax.broadcasted_iota(jnp.int32, (rows, rows), 0)
    c = lax.broadcasted_iota(jnp.int32, (rows, rows), 1)
    tri = (c < r).astype(BF16)
    base = run_ref[...] + jnp.dot(tri, sel.astype(BF16), preferred_element_type=F32)
    ranks = [jnp.sum(jnp.where(hot, base, 0.0), axis=-1, keepdims=True) for hot in hots]
    run_ref[...] = run_ref[...] + jnp.sum(sel, axis=0, keepdims=True)
    cnt_ref[...] = run_ref[...]
    lane_k = lax.broadcasted_iota(jnp.int32, (rows, TOP_K), 1)

    def pack(cols):
        out = jnp.broadcast_to(cols[TOP_K - 1], (rows, TOP_K))
        for k in range(TOP_K - 2, -1, -1):
            out = jnp.where(lane_k == k, cols[k], out)
        return out

    ti_ref[...] = pack(idxs)
    gate_ref[...] = pack(gates)
    rank_ref[...] = pack(ranks).astype(jnp.int32)


def _pool_kernel(h_ref, pw_ref, ps_ref, g_ref, b_ref, rw_ref, rb_ref,
                 h1_ref, ti_ref, gate_ref, rank_ref, cnt_ref,
                 ext_ref, run_ref, *, tile, alpha):
    bi = pl.program_id(0)
    i = pl.program_id(1)
    d_model = h_ref.shape[1]
    gdim = d_model // len(POOL_WINDOWS)

    @pl.when(i == 0)
    def _():
        ext_ref[0:MAX_WIN, :] = jnp.zeros((MAX_WIN, d_model), F32)

    @pl.when((bi == 0) & (i == 0))
    def _():
        run_ref[...] = jnp.zeros_like(run_ref)

    x = h_ref[...]
    ext_ref[MAX_WIN:MAX_WIN + tile, :] = x
    pos = i * tile + lax.broadcasted_iota(jnp.int32, (tile, 1), 0)
    ys = []
    for g, w in enumerate(POOL_WINDOWS):
        lo, hi = g * gdim, (g + 1) * gdim
        xg = x[:, lo:hi]
        s = xg
        for j in range(1, w):
            s = s + ext_ref[MAX_WIN - j:MAX_WIN - j + tile, lo:hi]
        cnt = jnp.minimum(pos + 1, w).astype(F32)
        u = s / cnt - xg
        ys.append(jnp.dot(u.astype(BF16), pw_ref[g].astype(BF16), preferred_element_type=F32))
    y = jnp.concatenate(ys, axis=-1) * ps_ref[...]
    hn = _layer_norm(alpha * x + y, g_ref[...], b_ref[...])
    h1_ref[...] = hn
    ext_ref[0:MAX_WIN, :] = x[tile - MAX_WIN:tile, :]
    _route(hn, rw_ref, rb_ref, run_ref, ti_ref, gate_ref, rank_ref, cnt_ref)


def _route_out_shapes(tp, n_exp):
    return [jax.ShapeDtypeStruct((tp, TOP_K), jnp.int32),
            jax.ShapeDtypeStruct((tp, TOP_K), F32),
            jax.ShapeDtypeStruct((tp, TOP_K), jnp.int32),
            jax.ShapeDtypeStruct((1, n_exp), F32)]


def _route_out_specs(nt, tile, n_exp):
    row = lambda b, i: (b * nt + i, 0)
    return [pl.BlockSpec((tile, TOP_K), row),
            pl.BlockSpec((tile, TOP_K), row),
            pl.BlockSpec((tile, TOP_K), row),
            pl.BlockSpec((1, n_exp), lambda b, i: (0, 0))]


def _const_spec(shape):
    return pl.BlockSpec(shape, lambda b, i: (0,) * len(shape))


def _pool_call(h, pool_w, pool_scale, ln_g, ln_b, router_w, router_b, *, bsz, lp, alpha):
    tp, d = h.shape
    tile = SEQ_TILE
    nt = lp // tile
    n_exp = router_w.shape[1]
    groups, gdim, _ = pool_w.shape
    row = lambda b, i: (b * nt + i, 0)
    return pl.pallas_call(
        functools.partial(_pool_kernel, tile=tile, alpha=alpha),
        grid=(bsz, nt),
        in_specs=[pl.BlockSpec((tile, d), row),
                  _const_spec((groups, gdim, gdim)),
                  _const_spec((1, d)), _const_spec((1, d)), _const_spec((1, d)),
                  _const_spec((d, n_exp)), _const_spec((1, n_exp))],
        out_specs=[pl.BlockSpec((tile, d), row)] + _route_out_specs(nt, tile, n_exp),
        out_shape=[jax.ShapeDtypeStruct((tp, d), F32)] + _route_out_shapes(tp, n_exp),
        scratch_shapes=[pltpu.VMEM((MAX_WIN + tile, d), F32), pltpu.VMEM((1, n_exp), F32)],
        compiler_params=_params(("arbitrary", "arbitrary")),
    )(h, pool_w, pool_scale.reshape(1, d), ln_g.reshape(1, d), ln_b.reshape(1, d),
      router_w, router_b.reshape(1, n_exp))


def _plan(counts, top_i, rank, n_blocks):
    n_exp = counts.shape[1]
    cnt = counts.reshape(n_exp).astype(jnp.int32)
    padded = (cnt + EXPERT_ROWS - 1) // EXPERT_ROWS * EXPERT_ROWS
    pad_ends = jnp.cumsum(padded)
    pad_starts = pad_ends - padded
    hot = top_i[..., None] == jnp.arange(n_exp, dtype=jnp.int32)
    dest = (jnp.sum(jnp.where(hot, pad_starts, 0), axis=-1) + rank).reshape(-1)
    starts = jnp.arange(n_blocks, dtype=jnp.int32) * EXPERT_ROWS
    block_e = jnp.sum((starts[:, None] >= pad_ends[None, :]).astype(jnp.int32), axis=1)
    block_e = jnp.minimum(block_e, n_exp - 1)
    n_used = (pad_ends[-1] // EXPERT_ROWS).reshape(1)
    return block_e, n_used, pad_starts + cnt, pad_ends, dest.reshape(-1, 1, SEQ_TILE * TOP_K)


def _scatter_plan(row_copy, tp):
    n_rows = row_copy.shape[0]
    valid = row_copy >= 0
    tok, k = row_copy // TOP_K, row_copy % TOP_K
    scrap = tp * TOP_K + jnp.cumsum(jnp.logical_not(valid).astype(jnp.int32)) - 1
    row_dst = jnp.where(valid, k * tp + tok, scrap)
    first = n_rows + jnp.arange(EXPERT_ROWS, dtype=jnp.int32) % (EXPERT_ROWS // 2)
    dst_ext = jnp.concatenate([first, row_dst])
    return dst_ext.reshape(-1, 1, EXPERT_ROWS)


def _dispatch_kernel(nu_ref, tails_ref, ends_ref, dest_ref, h_ref, xs_ref, inv_ref, zero_ref, sem, *, n_blocks):
    i = pl.program_id(0)
    tile = h_ref.shape[0]
    rb = zero_ref.shape[0]
    n_exp = tails_ref.shape[0]

    @pl.when(i == 0)
    def _():
        def no_copy(row, carry):
            inv_ref[row] = -1
            return carry

        for e in range(n_exp):
            lax.fori_loop(tails_ref[e], ends_ref[e], no_copy, 0)
        lax.fori_loop(nu_ref[0] * rb, n_blocks * rb, no_copy, 0)
        zero_ref[...] = jnp.zeros_like(zero_ref)
        fills = []
        for e in range(n_exp):
            start = jnp.minimum(tails_ref[e] // 8 * 8, (n_blocks - 1) * rb)
            fills.append(pltpu.make_async_copy(zero_ref, xs_ref.at[pl.ds(pl.multiple_of(start, 8), rb), :], sem))
        for cp in fills:
            cp.start()
        for cp in fills:
            cp.wait()

        def fill_block(blk, carry):
            cp = pltpu.make_async_copy(zero_ref, xs_ref.at[pl.ds(pl.multiple_of(blk * rb, rb), rb), :], sem)
            cp.start()
            cp.wait()
            return carry

        lax.fori_loop(nu_ref[0], n_blocks, fill_block, 0)

    for r in range(tile):
        for k in range(TOP_K):
            row = dest_ref[0, 0, r * TOP_K + k]
            inv_ref[row] = (i * tile + r) * TOP_K + k
            pltpu.make_async_copy(h_ref.at[pl.ds(r, 1), :], xs_ref.at[pl.ds(row, 1), :], sem).start()
    for _ in range(TOP_K):
        pltpu.make_async_copy(h_ref, xs_ref.at[pl.ds(0, tile), :], sem).wait()


def _dispatch_call(n_used, tails, ends, dest, h, n_blocks):
    tp, d = h.shape
    tile = SEQ_TILE
    rb = EXPERT_ROWS
    grid_spec = pltpu.PrefetchScalarGridSpec(
        num_scalar_prefetch=3,
        grid=(tp // tile,),
        in_specs=[pl.BlockSpec((1, 1, tile * TOP_K), lambda i, nu, tl, en: (i, 0, 0), memory_space=pltpu.SMEM),
                  pl.BlockSpec((tile, d), lambda i, nu, tl, en: (i, 0))],
        out_specs=[pl.BlockSpec(memory_space=pl.ANY), pl.BlockSpec(memory_space=pltpu.SMEM)],
        scratch_shapes=[pltpu.VMEM((rb, d), F32), pltpu.SemaphoreType.DMA(())],
    )
    return pl.pallas_call(
        functools.partial(_dispatch_kernel, n_blocks=n_blocks),
        grid_spec=grid_spec,
        out_shape=[jax.ShapeDtypeStruct((n_blocks * rb, d), F32),
                   jax.ShapeDtypeStruct((n_blocks * rb,), jnp.int32)],
        compiler_params=_params(("arbitrary",)),
    )(n_used, tails, ends, dest, h)


def _ffn_kernel(be_ref, nu_ref, dstp_ref, dst_ref, x_ref, w1_ref, b1_ref, w2_ref, b2_ref,
                y_ref, ya_ref, yb_ref, w1b_ref, w2b_ref, ssem):
    i = pl.program_id(0)
    half = ya_ref.shape[0]
    f = w2b_ref.shape[0]

    def scatter(idx_ref, base, buf_ref, sem):
        for r in range(half):
            pltpu.make_async_copy(buf_ref.at[pl.ds(r, 1), :],
                                  y_ref.at[pl.ds(idx_ref[0, 0, base + r], 1), :], sem).start()

    def wait_scatter(buf_ref, sem):
        pltpu.make_async_copy(buf_ref, y_ref.at[pl.ds(0, half), :], sem).wait()

    def ffn(x, out_ref):
        hid = jnp.dot(x.astype(BF16), w1b_ref[...], preferred_element_type=F32) + b1_ref[0]
        gate = jnp.minimum(hid[:, :f], SWIGLU_LIMIT)
        up = jnp.clip(hid[:, f:], -SWIGLU_LIMIT, SWIGLU_LIMIT)
        act = gate * jax.nn.sigmoid(SWIGLU_ALPHA * gate) * (up + 1.0)
        out_ref[...] = jnp.dot(act.astype(BF16), w2b_ref[...], preferred_element_type=F32) + b2_ref[0]

    @pl.when(i < nu_ref[0])
    def _():
        @pl.when(i == 0)
        def _():
            yb_ref[...] = jnp.zeros_like(yb_ref)

        @pl.when((i == 0) | (be_ref[i] != be_ref[jnp.maximum(i - 1, 0)]))
        def _():
            w1b_ref[...] = w1_ref[0, 0].astype(BF16)
            w2b_ref[...] = w2_ref[0, 0].astype(BF16)

        @pl.when(i >= 1)
        def _():
            wait_scatter(ya_ref, ssem.at[0])

        scatter(dstp_ref, half, yb_ref, ssem.at[1])
        ffn(x_ref[0:half, :], ya_ref)

        wait_scatter(yb_ref, ssem.at[1])
        scatter(dst_ref, 0, ya_ref, ssem.at[0])
        ffn(x_ref[half:2 * half, :], yb_ref)

        @pl.when(i == nu_ref[0] - 1)
        def _():
            scatter(dst_ref, half, yb_ref, ssem.at[1])
            wait_scatter(ya_ref, ssem.at[0])
            wait_scatter(yb_ref, ssem.at[1])

    @pl.when(i >= nu_ref[0])
    def _():
        ya_ref[...] = jnp.zeros_like(ya_ref)
        first = pl.multiple_of(dst_ref[0, 0, 0], 8)
        copies = [pltpu.make_async_copy(ya_ref, y_ref.at[pl.ds(first + j * half, half), :], ssem.at[j])
                  for j in range(2)]
        for cp in copies:
            cp.start()
        for cp in copies:
            cp.wait()


def _ffn_call(block_e, n_used, dst_ext, xs, layer, w1, b1, w2, b2):
    n_rows, d = xs.shape
    depth, n_exp, _, f2 = w1.shape
    f = w2.shape[2]
    rb = EXPERT_ROWS
    nb = n_rows // rb
    half = rb // 2
    n_out = n_rows + half

    def blk(i, be, nu):
        return jnp.minimum(i, nu[0] - 1)

    def idx_spec(index_map):
        return pl.BlockSpec((1, 1, rb), index_map, memory_space=pltpu.SMEM)

    grid_spec = pltpu.PrefetchScalarGridSpec(
        num_scalar_prefetch=2,
        grid=(nb,),
        in_specs=[idx_spec(lambda i, be, nu: (i, 0, 0)),
                  idx_spec(lambda i, be, nu: (i + 1, 0, 0)),
                  pl.BlockSpec((rb, d), lambda i, be, nu: (blk(i, be, nu), 0)),
                  pl.BlockSpec((1, 1, d, f2), lambda i, be, nu: (layer, be[blk(i, be, nu)], 0, 0)),
                  pl.BlockSpec((1, 1, f2), lambda i, be, nu: (layer * n_exp + be[blk(i, be, nu)], 0, 0)),
                  pl.BlockSpec((1, 1, f, d), lambda i, be, nu: (layer, be[blk(i, be, nu)], 0, 0)),
                  pl.BlockSpec((1, 1, d), lambda i, be, nu: (layer * n_exp + be[blk(i, be, nu)], 0, 0))],
        out_specs=pl.BlockSpec(memory_space=pl.ANY),
        scratch_shapes=[pltpu.VMEM((half, d), F32)] * 2 + [
            pltpu.VMEM((d, f2), BF16), pltpu.VMEM((f, d), BF16), pltpu.SemaphoreType.DMA((2,))],
    )
    return pl.pallas_call(
        _ffn_kernel,
        grid_spec=grid_spec,
        out_shape=jax.ShapeDtypeStruct((n_out, d), F32),
        compiler_params=_params(("arbitrary",)),
    )(block_e, n_used, dst_ext, dst_ext, xs,
      w1, b1.reshape(depth * n_exp, 1, f2), w2, b2.reshape(depth * n_exp, 1, d))


def _combine_kernel(gate_ref, h_ref, g_ref, b_ref, *refs, alpha):
    y_refs, out_ref = refs[:TOP_K], refs[TOP_K]
    gates = gate_ref[...]
    ffn = gates[:, 0:1] * y_refs[0][...]
    for k in range(1, TOP_K):
        ffn = ffn + gates[:, k:k + 1] * y_refs[k][...]
    out_ref[...] = _layer_norm(alpha * h_ref[...] + ffn, g_ref[...], b_ref[...])


def _combine_call(gates, h, y, ln_g, ln_b, *, alpha):
    tp, d = h.shape
    tile = ROUTE_TILE
    nt = tp // tile
    y_specs = [pl.BlockSpec((tile, d), functools.partial(lambda k, i: (k * nt + i, 0), k)) for k in range(TOP_K)]
    return pl.pallas_call(
        functools.partial(_combine_kernel, alpha=alpha),
        grid=(nt,),
        in_specs=[pl.BlockSpec((tile, TOP_K), lambda i: (i, 0)),
                  pl.BlockSpec((tile, d), lambda i: (i, 0)),
                  pl.BlockSpec((1, d), lambda i: (0, 0)),
                  pl.BlockSpec((1, d), lambda i: (0, 0))] + y_specs,
        out_specs=pl.BlockSpec((tile, d), lambda i: (i, 0)),
        out_shape=jax.ShapeDtypeStruct((tp, d), F32),
        compiler_params=_params(("arbitrary",)),
    )(gates, h, ln_g.reshape(1, d), ln_b.reshape(1, d), *([y] * TOP_K))


def _moe(h, top_i, gates, rank, counts, layer, w1, b1, w2, b2, ln_g, ln_b, *, alpha):
    tp = h.shape[0]
    n_exp = w1.shape[1]
    n_blocks = tp * TOP_K // EXPERT_ROWS + n_exp
    block_e, n_used, tails, ends, dest = _plan(counts, top_i, rank, n_blocks)
    xs, row_copy = _dispatch_call(n_used, tails, ends, dest, h, n_blocks)
    dst_ext = _scatter_plan(row_copy, tp)
    y = _ffn_call(block_e, n_used, dst_ext, xs, layer, w1, b1, w2, b2)
    return _combine_call(gates, h, y, ln_g, ln_b, alpha=alpha)


def _log_sigmoid(x):
    return jnp.minimum(x, 0.0) - jnp.log(1.0 + jnp.exp(-jnp.abs(x)))


def _split3(c):
    hi = c.astype(BF16)
    r1 = c - hi.astype(F32)
    mid = r1.astype(BF16)
    lo = (r1 - mid.astype(F32)).astype(BF16)
    return hi, mid, lo


def _proj_kernel(h_ref, wqt_ref, wk_ref, wvt_ref, wf_ref, bf_ref, selk_ref, onek_ref,
                 qt_ref, kx_ref, vt_ref, carry_ref, *, tile, n_heads, head_dim):
    i = pl.program_id(1)

    @pl.when(i == 0)
    def _():
        carry_ref[...] = jnp.zeros_like(carry_ref)

    x = h_ref[...]
    xb = x.astype(BF16)
    nt_dims = (((1,), (1,)), ((), ()))
    lf = _log_sigmoid(_dot3(x, wf_ref[...]) + bf_ref[...])
    r = lax.broadcasted_iota(jnp.int32, (tile, tile), 0)
    c = lax.broadcasted_iota(jnp.int32, (tile, tile), 1)
    tril = (c <= r).astype(BF16)
    cs = carry_ref[...]
    for part in _split3(lf):
        cs = cs + jnp.dot(tril, part, preferred_element_type=F32)
    carry_ref[...] = cs[tile - 1:tile, :]
    parts = _split3(cs * LOG2E)
    eye = (r == c).astype(BF16)
    parts_t = [lax.dot_general(p, eye, (((0,), (0,)), ((), ())), preferred_element_type=F32)
               for p in parts]

    kx = jnp.dot(xb, wk_ref[...], preferred_element_type=F32) + onek_ref[...]
    for p, part in enumerate(parts):
        kx = kx + jnp.dot(-part, selk_ref[p], preferred_element_type=F32)
    qt = lax.dot_general(wqt_ref[...], xb, nt_dims, preferred_element_type=F32)
    vt = lax.dot_general(wvt_ref[...], xb, nt_dims, preferred_element_type=F32)
    aug_rows = 16
    row = lax.broadcasted_iota(jnp.int32, (aug_rows, tile), 0)
    v_aug = jnp.where(row == 0, 1.0, 0.0).astype(BF16)
    rest = jnp.zeros((LANES - head_dim - aug_rows, tile), BF16)
    for hd in range(n_heads):
        lo = hd * head_dim
        q_aug = jnp.where(row < 6, 1.0, 0.0)
        for p in range(2, -1, -1):
            q_aug = jnp.where(row == p, parts_t[p][hd:hd + 1, :], q_aug)
        qt_ref[0, hd, 0, 0:head_dim, :] = qt[lo:lo + head_dim, :].astype(BF16)
        qt_ref[0, hd, 0, head_dim:head_dim + aug_rows, :] = q_aug.astype(BF16)
        qt_ref[0, hd, 0, head_dim + aug_rows:, :] = rest
        vt_ref[0, hd, 0, 0:head_dim, :] = vt[lo:lo + head_dim, :].astype(BF16)
        vt_ref[0, hd, 0, head_dim:head_dim + aug_rows, :] = v_aug
        vt_ref[0, hd, 0, head_dim + aug_rows:, :] = rest
        kx_ref[0, hd, 0] = kx[:, hd * LANES:(hd + 1) * LANES].astype(BF16)


def _attn_weights(w_in, b_f, n_heads, head_dim):
    d = w_in.shape[0]
    scale = head_dim ** -0.5 * LOG2E
    hw = n_heads * LANES
    wk = w_in[:, d:2 * d].reshape(d, n_heads, head_dim)
    wk = jnp.pad(wk, ((0, 0), (0, 0), (0, LANES - head_dim))).reshape(d, hw).astype(BF16)
    wqt = (w_in[:, :d] * scale).T.astype(BF16)
    wvt = w_in[:, 2 * d:3 * d].T.astype(BF16)
    selk = np.zeros((3, n_heads, hw), np.float32)
    onek = np.zeros((1, hw), np.float32)
    for h in range(n_heads):
        base = h * LANES + head_dim
        for p in range(3):
            selk[p, h, base + 3 + p] = 1.0
            onek[0, base + p] = 1.0
    return (wqt, wk, wvt, w_in[:, 3 * d:], b_f.reshape(1, n_heads), jnp.asarray(selk, BF16), jnp.asarray(onek))


def _proj_call(h, weights, *, bsz, lp, n_heads, head_dim):
    tp, d = h.shape
    tile = SEQ_TILE
    nt = lp // tile
    row = lambda b, i: (b * nt + i, 0)
    in_specs = [pl.BlockSpec((tile, d), row)] + [_const_spec(w.shape) for w in weights]
    t_spec = pl.BlockSpec((1, n_heads, 1, LANES, tile), lambda b, i: (b, 0, i, 0, 0))
    k_spec = pl.BlockSpec((1, n_heads, 1, tile, LANES), lambda b, i: (b, 0, i, 0, 0))
    return pl.pallas_call(
        functools.partial(_proj_kernel, tile=tile, n_heads=n_heads, head_dim=head_dim),
        grid=(bsz, nt),
        in_specs=in_specs,
        out_specs=[t_spec, k_spec, t_spec],
        out_shape=[jax.ShapeDtypeStruct((bsz, n_heads, nt, LANES, tile), BF16),
                   jax.ShapeDtypeStruct((bsz, n_heads, nt, tile, LANES), BF16),
                   jax.ShapeDtypeStruct((bsz, n_heads, nt, LANES, tile), BF16)],
        scratch_shapes=[pltpu.VMEM((1, n_heads), F32)],
        compiler_params=_params(("arbitrary", "arbitrary")),
    )(h, *weights)


def _attn_kernel(qt_ref, kx_ref, vt_ref, o_ref, s0_ref, s1_ref, s2_ref, p0_ref, p1_ref, p2_ref,
                 a0_ref, a1_ref, a2_ref, c0_ref, c1_ref, c2_ref, m_ref, acc_ref, *, nt, tile, head_dim):
    s_refs, p_refs, a_refs = (s0_ref, s1_ref, s2_ref), (p0_ref, p1_ref, p2_ref), (a0_ref, a1_ref, a2_ref)
    c_refs = (c0_ref, c1_ref, c2_ref)
    key = lax.broadcasted_iota(jnp.int32, (tile, tile), 0)
    qry = lax.broadcasted_iota(jnp.int32, (tile, tile), 1)

    def logits(slot, qi, kj):
        s = jnp.dot(kx_ref[0, 0, kj], qt_ref[0, 0, qi], preferred_element_type=F32)
        s_refs[slot][...] = s
        c_refs[slot][...] = jnp.max(s, axis=0, keepdims=True)

    def value_update(slot, kj, out_qi=None):
        acc = a_refs[slot][...] * acc_ref[...] + jnp.dot(
            vt_ref[0, 0, kj], p_refs[slot][...], preferred_element_type=F32)
        acc_ref[...] = acc
        if out_qi is not None:
            o_ref[0, 0, out_qi] = (acc[:head_dim, :] * (1.0 / acc[head_dim:head_dim + 1, :])).astype(BF16)

    def substep(slot, qi, kj, *, diag, write_out):
        if diag:
            nqi, nkj = qi + 1, jnp.int32(1)
        else:
            stay = qi - kj >= 2
            nqi, nkj = jnp.where(stay, qi, qi + 1), jnp.where(stay, kj + 2, 0)
        logits((slot + 2) % ATTN_DEPTH, jnp.minimum(nqi, nt - 1), nkj)
        pkj = jnp.where(kj >= 2, kj - 2, jnp.maximum(qi - 1 - jnp.where(kj == 0, 1, 0), 0))
        out_qi = jnp.where(kj == 1, qi - 1, qi) if write_out else None
        value_update((slot + 1) % ATTN_DEPTH, pkj, out_qi)
        m_old = m_ref[...]
        if diag:
            s = jnp.where(key <= qry, s_refs[slot][...], MASK_VALUE)
            m_new = jnp.maximum(m_old, jnp.max(s, axis=0, keepdims=True))
            p_refs[slot][...] = jnp.exp2(s - m_new).astype(BF16)
            m_ref[...] = jnp.full_like(m_old, MASK_VALUE)
        else:
            m_new = jnp.maximum(m_old, c_refs[slot][...])
            p_refs[slot][...] = jnp.exp2(s_refs[slot][...] - m_new).astype(BF16)
            m_ref[...] = m_new
        a_refs[slot][...] = jnp.exp2(m_old - m_new)

    def q_tile(slot, qi, n_loops, rem):
        def trip(i, carry):
            for j in range(ATTN_DEPTH):
                substep((slot + j) % ATTN_DEPTH, qi, ATTN_DEPTH * i + j, diag=False, write_out=j == 1)
            return carry

        lax.fori_loop(0, n_loops, trip, 0)
        for j in range(rem):
            substep((slot + j) % ATTN_DEPTH, qi, ATTN_DEPTH * n_loops + j, diag=False, write_out=j == 1)
        substep((slot + rem) % ATTN_DEPTH, qi, qi, diag=True, write_out=rem == 1)
        return (slot + rem + 1) % ATTN_DEPTH

    m_ref[...] = jnp.full_like(m_ref, MASK_VALUE)
    acc_ref[...] = jnp.ones_like(acc_ref)
    for slot in range(1, ATTN_DEPTH):
        p_refs[slot][...] = jnp.zeros_like(p_refs[slot])
        a_refs[slot][...] = jnp.ones_like(a_refs[slot])
    logits(0, 0, 0)
    logits(1, 1, 0)

    def group(g, carry):
        slot = 0
        for r in range(ATTN_DEPTH):
            slot = q_tile(slot, ATTN_DEPTH * g + r, g, r)
        assert slot == 0
        return carry

    lax.fori_loop(0, nt // ATTN_DEPTH, group, 0)
    slot = 0
    for qi in range(nt // ATTN_DEPTH * ATTN_DEPTH, nt):
        slot = q_tile(slot, jnp.int32(qi), jnp.int32(qi // ATTN_DEPTH), qi % ATTN_DEPTH)
    value_update((slot + 1) % ATTN_DEPTH, nt - 2)
    value_update((slot + 2) % ATTN_DEPTH, nt - 1, nt - 1)


def _attn_call(qt, kx, vt, *, head_dim):
    bsz, n_heads, nt, _, tile = qt.shape
    return pl.pallas_call(
        functools.partial(_attn_kernel, nt=nt, tile=tile, head_dim=head_dim),
        grid=(bsz, n_heads),
        in_specs=[pl.BlockSpec((1, 1, nt, LANES, tile), lambda b, h: (b, h, 0, 0, 0)),
                  pl.BlockSpec((1, 1, nt, tile, LANES), lambda b, h: (b, h, 0, 0, 0)),
                  pl.BlockSpec((1, 1, nt, LANES, tile), lambda b, h: (b, h, 0, 0, 0))],
        out_specs=pl.BlockSpec((1, 1, nt, head_dim, tile), lambda b, h: (b, h, 0, 0, 0)),
        out_shape=jax.ShapeDtypeStruct((bsz, n_heads, nt, head_dim, tile), BF16),
        scratch_shapes=([pltpu.VMEM((tile, tile), F32)] * ATTN_DEPTH + [pltpu.VMEM((tile, tile), BF16)] * ATTN_DEPTH
                        + [pltpu.VMEM((1, tile), F32)] * (2 * ATTN_DEPTH)
                        + [pltpu.VMEM((1, tile), F32), pltpu.VMEM((LANES, tile), F32)]),
        compiler_params=_params(("arbitrary", "arbitrary")),
    )(qt, kx, vt)


def _oproj_kernel(o_ref, wo_ref, h_ref, g_ref, b_ref, rw_ref, rb_ref,
                  hn_ref, ti_ref, gate_ref, rank_ref, cnt_ref, run_ref, *, alpha):
    bi = pl.program_id(0)
    i = pl.program_id(1)

    @pl.when((bi == 0) & (i == 0))
    def _():
        run_ref[...] = jnp.zeros_like(run_ref)

    n_heads, _, head_dim, tile = o_ref.shape[1:]
    o_t = o_ref[0].reshape(n_heads * head_dim, tile)
    att = lax.dot_general(o_t, wo_ref[...], (((0,), (0,)), ((), ())), preferred_element_type=F32)
    hn = _layer_norm(alpha * h_ref[...] + att, g_ref[...], b_ref[...])
    hn_ref[...] = hn
    _route(hn, rw_ref, rb_ref, run_ref, ti_ref, gate_ref, rank_ref, cnt_ref)


def _oproj_call(o, w_out, h, ln_g, ln_b, router_w, router_b, *, bsz, lp, alpha):
    tp, d = h.shape
    tile = SEQ_TILE
    nt = lp // tile
    n_exp = router_w.shape[1]
    row = lambda b, i: (b * nt + i, 0)
    return pl.pallas_call(
        functools.partial(_oproj_kernel, alpha=alpha),
        grid=(bsz, nt),
        in_specs=[pl.BlockSpec((1,) + o.shape[1:2] + (1,) + o.shape[3:], lambda b, i: (b, 0, i, 0, 0)),
                  _const_spec((d, d)),
                  pl.BlockSpec((tile, d), row),
                  _const_spec((1, d)), _const_spec((1, d)),
                  _const_spec((d, n_exp)), _const_spec((1, n_exp))],
        out_specs=[pl.BlockSpec((tile, d), row)] + _route_out_specs(nt, tile, n_exp),
        out_shape=[jax.ShapeDtypeStruct((tp, d), F32)] + _route_out_shapes(tp, n_exp),
        scratch_shapes=[pltpu.VMEM((1, n_exp), F32)],
        compiler_params=_params(("arbitrary", "arbitrary")),
    )(o, w_out.astype(BF16), h, ln_g.reshape(1, d), ln_b.reshape(1, d),
      router_w, router_b.reshape(1, n_exp))


def kernel(x, meta_tokens, pool_w, pool_scale, attn_w_in, attn_b_f, attn_w_out,
           ln_g, ln_b, router_w, router_b, w1, b1, w2, b2):
    bsz, seq, d = x.shape
    n_meta = meta_tokens.shape[0]
    depth = ln_g.shape[0]
    n_heads = attn_b_f.shape[-1]
    head_dim = d // n_heads
    alpha = float((2 * depth) ** 0.25)
    length = n_meta + seq
    lp = -(-length // SEQ_TILE) * SEQ_TILE
    assert d % (len(POOL_WINDOWS) * LANES) == 0 and head_dim % 16 == 0 and head_dim + 16 <= LANES and depth == 2

    meta = jnp.broadcast_to(meta_tokens[None], (bsz, n_meta, d))
    h = jnp.concatenate([meta, x, jnp.zeros((bsz, lp - length, d), x.dtype)], axis=1)
    h = h.reshape(bsz * lp, d)

    h, top_i, gates, rank, counts = _pool_call(
        h, pool_w[0], pool_scale[0], ln_g[0, 0], ln_b[0, 0], router_w[0], router_b[0],
        bsz=bsz, lp=lp, alpha=alpha)
    h = _moe(h, top_i, gates, rank, counts, 0, w1, b1, w2, b2, ln_g[0, 1], ln_b[0, 1], alpha=alpha)

    weights = _attn_weights(attn_w_in[0], attn_b_f[0], n_heads, head_dim)
    qt, kx, vt = _proj_call(h, weights, bsz=bsz, lp=lp, n_heads=n_heads, head_dim=head_dim)
    o = _attn_call(qt, kx, vt, head_dim=head_dim)
    h, top_i, gates, rank, counts = _oproj_call(
        o, attn_w_out[0], h, ln_g[1, 0], ln_b[1, 0], router_w[1], router_b[1],
        bsz=bsz, lp=lp, alpha=alpha)
    h = _moe(h, top_i, gates, rank, counts, 1, w1, b1, w2, b2, ln_g[1, 1], ln_b[1, 1], alpha=alpha)

    return h.reshape(bsz, lp, d)[:, n_meta:length]
```
